```python
import math
import jax, jax.numpy as jnp
from jax import lax
import numpy as np

D_MODEL = 4096
BATCH = 4
SEQ = 2048
DEPTH = 4
DEC_BATCH = 8
DEC_SEQ = 1
PAST_LEN = 8192
PAGE_SIZE = 128

W_SSD = D_MODEL // 4
W_POOL = D_MODEL // 4
W_RET = D_MODEL // 4
W_SB = D_MODEL - W_SSD - W_POOL - W_RET
SSD_HEAD_DIM = 64
SSD_HEADS = W_SSD // SSD_HEAD_DIM
SSD_GROUPS = 4
SSD_HPG = SSD_HEADS // SSD_GROUPS
SSD_STATE = 128
SSD_CONV = 4
SSD_CHUNK = 128
SSD_GN = SSD_GROUPS * SSD_STATE
SSD_CONV_CH = W_SSD + 2 * SSD_GN
POOL_WINDOWS = (2, 4, 8, 16)
POOL_GROUPS = len(POOL_WINDOWS)
POOL_GC = W_POOL // POOL_GROUPS
POOL_BUF = max(POOL_WINDOWS) - 1
RET_HEAD_DIM = 128
RET_HEADS = W_RET // RET_HEAD_DIM
RET_CHUNK = 128
ROPE_BASE = 10000.0
SB_HEAD_DIM = 128
SB_HEADS = W_SB // SB_HEAD_DIM
SB_BLOCK = 128
D_FF = 4 * D_MODEL
NORM_EPS = 1e-6
IN_SPLITS = (W_SSD, W_SSD, SSD_GN, SSD_GN, SSD_HEADS, W_POOL, W_RET, W_RET, W_RET, W_RET, W_SB, W_SB, W_SB)
IN_TOTAL = sum(IN_SPLITS)

kernel_name = 'hybrid_ssd_pool_retention_stickbreak_decode_step'


def rms_norm(x, w):
    xf = x.astype(jnp.float32)
    y = xf * lax.rsqrt(jnp.mean(xf * xf, axis=-1, keepdims=True) + NORM_EPS)
    return (y * w.astype(jnp.float32)).astype(x.dtype)


def rope(x, pos):
    half = x.shape[-1] // 2
    inv = ROPE_BASE ** (-jnp.arange(half, dtype=jnp.float32) / half)
    ang = pos.astype(jnp.float32)[:, None] * inv[None, :]
    cos = jnp.cos(ang)[None, :, None, :]
    sin = jnp.sin(ang)[None, :, None, :]
    x1, x2 = x[..., :half], x[..., half:]
    return jnp.concatenate([x1 * cos - x2 * sin, x1 * sin + x2 * cos], axis=-1)


def ssd_mixer(xs, z, bm, cm, dt_raw, conv_buf, ssm_state, conv_w, conv_b, dt_bias, a_log, d_skip, norm_w):
    b, L, _ = xs.shape
    f32 = jnp.float32
    xbc = jnp.concatenate([xs, bm, cm], axis=-1)
    ext = jnp.concatenate([conv_buf.astype(xbc.dtype), xbc], axis=1)
    new_buf = ext[:, L:]
    conv = lax.conv_general_dilated(ext, conv_w[:, None, :].astype(ext.dtype), window_strides=(1,), padding='VALID',
                                    dimension_numbers=('NWC', 'WIO', 'NWC'), feature_group_count=SSD_CONV_CH)
    xbc = jax.nn.silu(conv.astype(f32) + conv_b.astype(f32))
    xh = xbc[..., :W_SSD].reshape(b, L, SSD_GROUPS, SSD_HPG, SSD_HEAD_DIM)
    bg = xbc[..., W_SSD:W_SSD + SSD_GN].reshape(b, L, SSD_GROUPS, SSD_STATE)
    cg = xbc[..., W_SSD + SSD_GN:].reshape(b, L, SSD_GROUPS, SSD_STATE)
    dt = jax.nn.softplus(dt_raw.astype(f32) + dt_bias.astype(f32)).reshape(b, L, SSD_GROUPS, SSD_HPG)
    a = -jnp.exp(a_log.astype(f32)).reshape(SSD_GROUPS, SSD_HPG)
    da = dt * a
    xdt = xh * dt[..., None]
    chunk = SSD_CHUNK if L % SSD_CHUNK == 0 else L
    nc = L // chunk

    def to_chunks(t):
        return jnp.moveaxis(t.reshape(b, nc, chunk, *t.shape[2:]), 1, 0)

    causal = jnp.tril(jnp.ones((chunk, chunk), dtype=bool))[None, :, :, None, None]

    def step(S, inp):
        xc, bc, cc, dac = inp
        cum = jnp.cumsum(dac, axis=1)
        seg = jnp.where(causal, cum[:, :, None] - cum[:, None, :], -jnp.inf)
        decay = jnp.exp(seg)
        cb = jnp.einsum('bign,bjgn->bijg', cc, bc)
        y_in = jnp.einsum('bijg,bijgh,bjghp->bighp', cb, decay, xc)
        y_off = jnp.einsum('bign,bigh,bghpn->bighp', cc, jnp.exp(cum), S)
        end = cum[:, -1]
        w_end = jnp.exp(end[:, None] - cum)
        S_new = S * jnp.exp(end)[..., None, None] + jnp.einsum('bjgh,bjgn,bjghp->bghpn', w_end, bc, xc)
        return S_new, y_in + y_off

    S0 = ssm_state.astype(f32).reshape(b, SSD_GROUPS, SSD_HPG, SSD_HEAD_DIM, SSD_STATE)
    S_fin, y_c = lax.scan(step, S0, (to_chunks(xdt), to_chunks(bg), to_chunks(cg), to_chunks(da)))
    y = jnp.moveaxis(y_c, 0, 1).reshape(b, L, SSD_GROUPS, SSD_HPG, SSD_HEAD_DIM)
    y = y + d_skip.astype(f32).reshape(SSD_GROUPS, SSD_HPG)[:, :, None] * xh
    y = y.reshape(b, L, W_SSD) * jax.nn.silu(z.astype(f32))
    y = rms_norm(y, norm_w)
    new_state = S_fin.reshape(b, SSD_HEADS, SSD_HEAD_DIM, SSD_STATE).astype(ssm_state.dtype)
    return y.astype(xs.dtype), new_buf, new_state


def pool_mixer(u, pool_buf, start, pool_w, pool_scale):
    b, L, _ = u.shape
    f32 = jnp.float32
    ext = jnp.concatenate([pool_buf.astype(u.dtype), u], axis=1)
    new_buf = ext[:, L:]
    cs = jnp.cumsum(ext.astype(f32), axis=1)
    cs = jnp.concatenate([jnp.zeros((b, 1, W_POOL), f32), cs], axis=1)
    pos = start + jnp.arange(L)
    means = []
    for g, w in enumerate(POOL_WINDOWS):
        lo_c, hi_c = g * POOL_GC, (g + 1) * POOL_GC
        hi = cs[:, POOL_BUF + 1:POOL_BUF + 1 + L, lo_c:hi_c]
        lo = cs[:, POOL_BUF + 1 - w:POOL_BUF + 1 - w + L, lo_c:hi_c]
        cnt = jnp.minimum(pos + 1, w).astype(f32)
        means.append((hi - lo) / cnt[None, :, None])
    mean = jnp.concatenate(means, axis=-1)
    d = (mean - u.astype(f32)).reshape(b, L, POOL_GROUPS, POOL_GC).astype(u.dtype)
    y = jnp.einsum('blgc,gcd->blgd', d, pool_w).reshape(b, L, W_POOL)
    return (y * pool_scale).astype(u.dtype), new_buf


def retention_mixer(q, k, v, g, ret_state, start, gn_w):
    b, L, _ = q.shape
    f32 = jnp.float32
    pos = start + jnp.arange(L)
    qh = rope(q.astype(f32).reshape(b, L, RET_HEADS, RET_HEAD_DIM), pos) * (RET_HEAD_DIM ** -0.5)
    kh = rope(k.astype(f32).reshape(b, L, RET_HEADS, RET_HEAD_DIM), pos)
    vh = v.astype(f32).reshape(b, L, RET_HEADS, RET_HEAD_DIM)
    log_gamma = jnp.log1p(-jnp.exp2(-5.0 - jnp.arange(RET_HEADS, dtype=f32)))
    chunk = RET_CHUNK if L % RET_CHUNK == 0 else L
    nc = L // chunk
    idx = jnp.arange(chunk, dtype=f32)
    diff = idx[:, None] - idx[None, :]
    causal = diff >= 0
    intra = jnp.where(causal[None], jnp.exp(log_gamma[:, None, None] * jnp.where(causal, diff, 0.0)[None]), 0.0)
    q_dec = jnp.exp(log_gamma[:, None] * (idx + 1.0)[None, :])
    k_dec = jnp.exp(log_gamma[:, None] * (chunk - 1.0 - idx)[None, :])
    c_dec = jnp.exp(log_gamma * chunk)

    def to_chunks(t):
        return jnp.moveaxis(t.reshape(b, nc, chunk, RET_HEADS, RET_HEAD_DIM), 1, 0)

    def step(S, inp):
        qc, kc, vc = inp
        s = jnp.einsum('bihd,bjhd->bhij', qc, kc) * intra[None]
        y = jnp.einsum('bhij,bjhd->bihd', s, vc)
        y = y + jnp.einsum('bihd,hi,bhde->bihe', qc, q_dec, S)
        S_new = S * c_dec[None, :, None, None] + jnp.einsum('bjhd,hj,bjhe->bhde', kc, k_dec, vc)
        return S_new, y

    S_fin, y_c = lax.scan(step, ret_state.astype(f32), (to_chunks(qh), to_chunks(kh), to_chunks(vh)))
    y = jnp.moveaxis(y_c, 0, 1).reshape(b, L, RET_HEADS, RET_HEAD_DIM)
    mu = jnp.mean(y, axis=-1, keepdims=True)
    var = jnp.mean(jnp.square(y - mu), axis=-1, keepdims=True)
    y = ((y - mu) * lax.rsqrt(var + NORM_EPS)).reshape(b, L, W_RET) * gn_w.astype(f32)
    y = jax.nn.silu(g.astype(f32)) * y
    return y.astype(q.dtype), S_fin.astype(ret_state.dtype)


def stick_breaking(q, k, v, q_start):
    b, Lq, H, d = q.shape
    Lk = k.shape[1]
    blk = SB_BLOCK if Lq % SB_BLOCK == 0 else Lq
    nb = Lq // blk
    qb = jnp.moveaxis(q.reshape(b, nb, blk, H, d), 1, 0)
    starts = q_start + blk * jnp.arange(nb)
    key_pos = jnp.arange(Lk)
    scale = d ** -0.5

    def one_block(args):
        qi, s0 = args
        qpos = s0 + jnp.arange(blk)
        zz = jnp.einsum('bihd,bshd->bhis', qi, k).astype(jnp.float32) * scale
        mask = key_pos[None, :] < qpos[:, None]
        log_beta = jax.nn.log_sigmoid(zz)
        log_1mb = jnp.where(mask, jax.nn.log_sigmoid(-zz), 0.0)
        cs = jnp.cumsum(log_1mb, axis=-1)
        log_a = log_beta + (cs[..., -1:] - cs)
        a = jnp.where(mask, jnp.exp(log_a), 0.0)
        return jnp.einsum('bhis,bshd->bihd', a.astype(v.dtype), v)

    out = lax.map(one_block, (qb, starts))
    return jnp.moveaxis(out, 0, 1).reshape(b, Lq, H, d)


def trunk_layer(x, past_k, past_v, conv_buf, ssm_state, pool_buf, ret_state,
                norm1_w, w_in, conv_w, conv_b, dt_bias, a_log, d_skip, ssd_norm_w, pool_w, pool_scale,
                ret_gn_w, sb_q_norm, sb_k_norm, w_out, norm2_w, w_ff1, w_ff2):
    b, L, _ = x.shape
    start = 0 if past_k is None else past_k.shape[1]
    h = rms_norm(x, norm1_w)
    proj = jnp.einsum('bld,de->ble', h, w_in)
    offs = np.cumsum(IN_SPLITS)[:-1].tolist()
    xs, z, bm, cm, dt_raw, u, rq, rk, rv, rg, sq, sk, sv = jnp.split(proj, offs, axis=-1)
    y_ssd, conv_new, ssm_new = ssd_mixer(xs, z, bm, cm, dt_raw, conv_buf, ssm_state, conv_w, conv_b, dt_bias,
                                         a_log, d_skip, ssd_norm_w)
    y_pool, pool_new = pool_mixer(u, pool_buf, start, pool_w, pool_scale)
    y_ret, ret_new = retention_mixer(rq, rk, rv, rg, ret_state, start, ret_gn_w)
    q = rms_norm(sq.reshape(b, L, SB_HEADS, SB_HEAD_DIM), sb_q_norm)
    k = rms_norm(sk.reshape(b, L, SB_HEADS, SB_HEAD_DIM), sb_k_norm)
    v = sv.reshape(b, L, SB_HEADS, SB_HEAD_DIM)
    if past_k is None:
        k_all, v_all = k, v
    else:
        k_all = jnp.concatenate([past_k.astype(k.dtype), k], axis=1)
        v_all = jnp.concatenate([past_v.astype(v.dtype), v], axis=1)
    y_sb = stick_breaking(q, k_all, v_all, start).reshape(b, L, W_SB)
    mix = jnp.concatenate([y_ssd, y_pool, y_ret, y_sb], axis=-1)
    x = x + jnp.einsum('ble,ed->bld', mix, w_out)
    h2 = rms_norm(x, norm2_w)
    ff = jnp.square(jax.nn.relu(jnp.einsum('bld,df->blf', h2, w_ff1)))
    x = x + jnp.einsum('blf,fd->bld', ff, w_ff2)
    return x, k, v, conv_new, ssm_new, pool_new, ret_new


def setup_inputs(seed: int = 0) -> dict:
    key = jax.random.key(seed)
    ks = jax.random.split(key, 32)
    f32 = jnp.float32
    n_pages = PAST_LEN // PAGE_SIZE
    n_phys = -(-5 * DEC_BATCH * n_pages // 4)
    nrm = lambda k, shape, s=1.0: jax.random.normal(k, shape, f32) * s
    gain = lambda k, shape: 1.0 + 0.01 * jax.random.normal(k, shape, f32)
    perm = jax.random.permutation(ks[8], n_phys)
    page_table = perm[:DEC_BATCH * n_pages].reshape(DEC_BATCH, n_pages).astype(jnp.int32)
    dt0 = jnp.exp(jax.random.uniform(ks[12], (DEPTH, SSD_HEADS), f32, math.log(1e-3), math.log(1e-1)))
    dt_bias = dt0 + jnp.log(-jnp.expm1(-dt0))
    a_log = jnp.log(jax.random.uniform(ks[13], (DEPTH, SSD_HEADS), f32, 1.0, 16.0))
    return {
        'x_prompt': nrm(ks[0], (BATCH, SEQ, D_MODEL)),
        'x_sample': nrm(ks[1], (DEC_BATCH, DEC_SEQ, D_MODEL)),
        'cache_k': nrm(ks[2], (DEPTH, n_phys, PAGE_SIZE, SB_HEADS, SB_HEAD_DIM)),
        'cache_v': nrm(ks[3], (DEPTH, n_phys, PAGE_SIZE, SB_HEADS, SB_HEAD_DIM)),
        'state_ssm': nrm(ks[4], (DEPTH, DEC_BATCH, SSD_HEADS, SSD_HEAD_DIM, SSD_STATE), 0.1),
        'state_conv': nrm(ks[5], (DEPTH, DEC_BATCH, SSD_CONV - 1, SSD_CONV_CH)),
        'state_pool': nrm(ks[6], (DEPTH, DEC_BATCH, POOL_BUF, W_POOL)),
        'state_ret': nrm(ks[7], (DEPTH, DEC_BATCH, RET_HEADS, RET_HEAD_DIM, RET_HEAD_DIM), 0.1),
        'page_table': page_table,
        'norm1_w': gain(ks[9], (DEPTH, D_MODEL)),
        'w_in': nrm(ks[10], (DEPTH, D_MODEL, IN_TOTAL), D_MODEL ** -0.5),
        'conv_w': nrm(ks[11], (DEPTH, SSD_CONV, SSD_CONV_CH), SSD_CONV ** -0.5),
        'conv_b': nrm(ks[14], (DEPTH, SSD_CONV_CH), 0.01),
        'dt_bias': dt_bias,
        'a_log': a_log,
        'd_skip': gain(ks[15], (DEPTH, SSD_HEADS)),
        'ssd_norm_w': gain(ks[16], (DEPTH, W_SSD)),
        'pool_w': nrm(ks[17], (DEPTH, POOL_GROUPS, POOL_GC, POOL_GC), POOL_GC ** -0.5),
        'pool_scale': gain(ks[18], (DEPTH, W_POOL)),
        'ret_gn_w': gain(ks[19], (DEPTH, W_RET)),
        'sb_q_norm': gain(ks[20], (DEPTH, SB_HEAD_DIM)),
        'sb_k_norm': gain(ks[21], (DEPTH, SB_HEAD_DIM)),
        'w_out': nrm(ks[22], (DEPTH, D_MODEL, D_MODEL), D_MODEL ** -0.5),
        'norm2_w': gain(ks[23], (DEPTH, D_MODEL)),
        'w_ff1': nrm(ks[24], (DEPTH, D_MODEL, D_FF), D_MODEL ** -0.5),
        'w_ff2': nrm(ks[25], (DEPTH, D_FF, D_MODEL), D_FF ** -0.5),
    }


def reference(x_prompt, x_sample, cache_k, cache_v, state_ssm, state_conv, state_pool, state_ret, page_table,
              norm1_w, w_in, conv_w, conv_b, dt_bias, a_log, d_skip, ssd_norm_w, pool_w, pool_scale,
              ret_gn_w, sb_q_norm, sb_k_norm, w_out, norm2_w, w_ff1, w_ff2):
    bp = x_prompt.shape[0]
    bs = x_sample.shape[0]
    dt = x_prompt.dtype
    yp, ys = x_prompt, x_sample
    kp_l, vp_l, ks_l, vs_l = [], [], [], []
    sp_l, ss_l, cp_l, cs_l, pp_l, ps_l, rp_l, rs_l = [], [], [], [], [], [], [], []
    for l in range(DEPTH):
        weights = (norm1_w[l], w_in[l], conv_w[l], conv_b[l], dt_bias[l], a_log[l], d_skip[l], ssd_norm_w[l],
                   pool_w[l], pool_scale[l], ret_gn_w[l], sb_q_norm[l], sb_k_norm[l], w_out[l], norm2_w[l],
                   w_ff1[l], w_ff2[l])
        yp, kp, vp, cp, sp, pp, rp = trunk_layer(
            yp, None, None,
            jnp.zeros((bp, SSD_CONV - 1, SSD_CONV_CH), dt),
            jnp.zeros((bp, SSD_HEADS, SSD_HEAD_DIM, SSD_STATE), dt),
            jnp.zeros((bp, POOL_BUF, W_POOL), dt),
            jnp.zeros((bp, RET_HEADS, RET_HEAD_DIM, RET_HEAD_DIM), dt),
            *weights)
        past_k = cache_k[l][page_table].reshape(bs, -1, SB_HEADS, SB_HEAD_DIM)
        past_v = cache_v[l][page_table].reshape(bs, -1, SB_HEADS, SB_HEAD_DIM)
        ys, ksn, vsn, csn, ssn, psn, rsn = trunk_layer(
            ys, past_k, past_v, state_conv[l], state_ssm[l], state_pool[l], state_ret[l], *weights)
        kp_l.append(kp); vp_l.append(vp); ks_l.append(ksn); vs_l.append(vsn)
        sp_l.append(sp); ss_l.append(ssn); cp_l.append(cp); cs_l.append(csn)
        pp_l.append(pp); ps_l.append(psn); rp_l.append(rp); rs_l.append(rsn)
    new_k_prompt = jnp.stack(kp_l)
    new_v_prompt = jnp.stack(vp_l)
    new_k_sample = jnp.stack(ks_l)
    new_v_sample = jnp.stack(vs_l)
    ssm_prompt = jnp.stack(sp_l)
    ssm_sample = jnp.stack(ss_l)
    conv_prompt = jnp.stack(cp_l)
    conv_sample = jnp.stack(cs_l)
    pool_prompt = jnp.stack(pp_l)
    pool_sample = jnp.stack(ps_l)
    ret_prompt = jnp.stack(rp_l)
    ret_sample = jnp.stack(rs_l)
    return (yp, ys, new_k_prompt, new_v_prompt, new_k_sample, new_v_sample, ssm_prompt, ssm_sample,
            conv_prompt, conv_sample, pool_prompt, pool_sample, ret_prompt, ret_sample)
```

```python
import functools
import math

import jax
import jax.numpy as jnp
from jax import lax
from jax.experimental import pallas as pl
from jax.experimental.pallas import tpu as pltpu

F32 = jnp.float32
BF16 = jnp.bfloat16

D_MODEL = 4096
DEPTH = 4
PAGE_SIZE = 128
W_SSD = W_POOL = W_RET = W_SB = 1024
SSD_HEADS = 16
SSD_GROUPS = 4
SSD_STATE = 128
SSD_GN = SSD_GROUPS * SSD_STATE
SSD_CONV = 4
SSD_CONV_CH = W_SSD + 2 * SSD_GN
POOL_WINDOWS = (2, 4, 8, 16)
POOL_GC = W_POOL // len(POOL_WINDOWS)
POOL_BUF = max(POOL_WINDOWS) - 1
RET_HEADS = 8
HEAD_DIM = 128
SB_HEADS = 8
ROPE_BASE = 10000.0
D_FF = 4 * D_MODEL
NORM_EPS = 1e-6
CHUNK = 128
N_PROJ = 11 * 1024
DT_OFF = 3 * 1024
LANES = 128
NEG_BIG = -1e30
VMEM_LIMIT = 48 * 1024 * 1024

_NT = (((1,), (1,)), ((), ()))


def _params(sem):
    return pltpu.CompilerParams(dimension_semantics=sem, vmem_limit_bytes=VMEM_LIMIT)


def _silu(x):
    return x / (1.0 + jnp.exp(-x))


def _softplus(x):
    return jnp.maximum(x, 0.0) + jnp.log1p(jnp.exp(-jnp.abs(x)))


def _dot(a, b):
    return jnp.dot(a.astype(BF16), b.astype(BF16), preferred_element_type=F32)


def _dot_nt(a, b):
    return lax.dot_general(a.astype(BF16), b.astype(BF16), _NT, preferred_element_type=F32)


def _rmsnorm_kernel(x_ref, w_ref, o_ref):
    x = x_ref[...]
    ms = jnp.mean(x * x, axis=-1, keepdims=True)
    o_ref[...] = (x * lax.rsqrt(ms + NORM_EPS) * w_ref[...]).astype(o_ref.dtype)


def rmsnorm_bf16(x, w):
    m, d = x.shape
    tr = min(m, 256)
    return pl.pallas_call(
        _rmsnorm_kernel,
        grid=(m // tr,),
        in_specs=[pl.BlockSpec((tr, d), lambda i: (i, 0)), pl.BlockSpec((1, d), lambda i: (0, 0))],
        out_specs=pl.BlockSpec((tr, d), lambda i: (i, 0)),
        out_shape=jax.ShapeDtypeStruct((m, d), BF16),
        compiler_params=_params(("parallel",)),
        name="rmsnorm",
    )(x, w.reshape(1, d))


def _mm_kernel(*refs, nk, epilogue):
    if epilogue == "residual":
        x_ref, w_ref, r_ref, o_ref = refs[:4]
        scratch = refs[4:]
    else:
        x_ref, w_ref, o_ref = refs[:3]
        r_ref = None
        scratch = refs[3:]

    def finish(acc):
        if epilogue == "relu2":
            r = jnp.maximum(acc, 0.0)
            acc = r * r
        elif epilogue == "residual":
            acc = acc + r_ref[...]
        o_ref[...] = acc.astype(o_ref.dtype)

    part = jnp.dot(x_ref[...], w_ref[...], preferred_element_type=F32)
    if nk == 1:
        finish(part)
        return
    (acc_ref,) = scratch
    k = pl.program_id(2)

    @pl.when(k == 0)
    def _():
        acc_ref[...] = part

    @pl.when(k > 0)
    def _():
        acc_ref[...] += part

    @pl.when(k == nk - 1)
    def _():
        finish(acc_ref[...])


def matmul(x, w, *, tm, tn, tk, out_dtype=F32, epilogue=None, residual=None, name="matmul"):
    m, kdim = x.shape
    n = w.shape[1]
    tm, tn, tk = min(tm, m), min(tn, n), min(tk, kdim)
    nk = kdim // tk
    in_specs = [
        pl.BlockSpec((tm, tk), lambda j, i, k: (i, k)),
        pl.BlockSpec((tk, tn), lambda j, i, k: (k, j)),
    ]
    args = [x, w]
    if epilogue == "residual":
        in_specs.append(pl.BlockSpec((tm, tn), lambda j, i, k: (i, j)))
        args.append(residual)
    return pl.pallas_call(
        functools.partial(_mm_kernel, nk=nk, epilogue=epilogue),
        grid=(n // tn, m // tm, nk),
        in_specs=in_specs,
        out_specs=pl.BlockSpec((tm, tn), lambda j, i, k: (i, j)),
        out_shape=jax.ShapeDtypeStruct((m, n), out_dtype),
        scratch_shapes=[] if nk == 1 else [pltpu.VMEM((tm, tn), F32)],
        compiler_params=_params(("parallel", "parallel", "arbitrary")),
        name=name,
    )(*args)


def _ssd_kernel(xs_ref, z_ref, bm_ref, cm_ref, dt_ref, cs_ref, s0_ref, cw_ref, cb_ref, dtb_ref, alog_ref,
                dsk_ref, nw_ref, y_ref, cnew_ref, snew_ref, ext_ref, s_ref, *, nc, n_valid):
    c = pl.program_id(1)
    C = CHUNK

    @pl.when(c == 0)
    def _():
        ext_ref[0:8, :] = cs_ref[0]
        s_ref[...] = s0_ref[0]

    ext_ref[8:8 + C, 0:W_SSD] = xs_ref[...]
    ext_ref[8:8 + C, W_SSD:W_SSD + SSD_GN] = bm_ref[...]
    ext_ref[8:8 + C, W_SSD + SSD_GN:] = cm_ref[...]
    cw = cw_ref[...]
    conv = cb_ref[...] + ext_ref[8:8 + C, :] * cw[3:4, :]
    for tap in range(SSD_CONV - 1):
        conv = conv + ext_ref[5 + tap:5 + tap + C, :] * cw[tap:tap + 1, :]

    @pl.when(c == nc - 1)
    def _():
        cnew_ref[0] = ext_ref[5 + n_valid:8 + n_valid, :]

    ext_ref[0:8, :] = ext_ref[C:C + 8, :]

    xbc = _silu(conv)
    x = xbc[:, :W_SSD]
    bmat = xbc[:, W_SSD:W_SSD + SSD_GN]
    cmat = xbc[:, W_SSD + SSD_GN:]

    dt = _softplus(dt_ref[...] + dtb_ref[...])
    if n_valid < C:
        dt = jnp.where(lax.broadcasted_iota(jnp.int32, (C, LANES), 0) < n_valid, dt, 0.0)
    da = dt * (-jnp.exp(alog_ref[...]))
    ri = lax.broadcasted_iota(jnp.int32, (C, C), 0)
    ci = lax.broadcasted_iota(jnp.int32, (C, C), 1)
    tril = ri >= ci
    cum = jnp.dot(tril.astype(F32), da, precision=lax.Precision.HIGHEST, preferred_element_type=F32)
    cum_t = cum.T
    end = cum[C - 1:C, :]
    e_end = jnp.exp(end)
    lo = lax.broadcasted_iota(jnp.int32, (C, LANES), 1) < 64
    row_lo = lax.broadcasted_iota(jnp.int32, (C, LANES), 0) < 64
    dsk = dsk_ref[...]

    def pick(v, h0):
        return jnp.where(lo, v[:, h0:h0 + 1], v[:, h0 + 1:h0 + 2])

    ys = []
    for g in range(SSD_GROUPS):
        bg = bmat[:, g * SSD_STATE:(g + 1) * SSD_STATE].astype(BF16)
        cg = cmat[:, g * SSD_STATE:(g + 1) * SSD_STATE].astype(BF16)
        cb = _dot_nt(cg, bg)
        for pr in range(2):
            p = 2 * g + pr
            h0 = 2 * p
            xp = x[:, p * LANES:(p + 1) * LANES]
            xdt = xp * pick(dt, h0)
            y_in = jnp.zeros((C, LANES), F32)
            for h, keep in ((h0, lo), (h0 + 1, jnp.logical_not(lo))):
                seg = cum[:, h:h + 1] - cum_t[h:h + 1, :]
                decay = jnp.exp(jnp.where(tril, seg, NEG_BIG))
                y_in = y_in + _dot(cb * decay, jnp.where(keep, xdt, 0.0))
            sp = s_ref[p]
            y_off = _dot_nt(cg, sp) * jnp.exp(pick(cum, h0))
            ys.append(y_in + y_off + jnp.where(lo[0:1], dsk[:, h0:h0 + 1], dsk[:, h0 + 1:h0 + 2]) * xp)
            w_end = jnp.exp(pick(jnp.broadcast_to(end, (C, LANES)) - cum, h0))
            s_scale = jnp.where(row_lo, e_end[:, h0:h0 + 1], e_end[:, h0 + 1:h0 + 2])
            s_ref[p] = sp * s_scale + _dot((xdt * w_end).T, bg)

    y = jnp.concatenate(ys, axis=1) * _silu(z_ref[...])
    ms = jnp.mean(y * y, axis=-1, keepdims=True)
    y_ref[...] = (y * lax.rsqrt(ms + NORM_EPS) * nw_ref[...]).astype(y_ref.dtype)

    @pl.when(c == nc - 1)
    def _():
        snew_ref[0] = s_ref[...]


def ssd_mixer(proj, dt_raw, conv_state8, ssm_state, conv_w, conv_b, dt_bias, a_log, d_skip, norm_w, *, nb, nc, n_valid):
    assert n_valid == CHUNK or nc == 1
    rows = nb * nc * CHUNK
    pad = lambda v: jnp.pad(v, (0, LANES - SSD_HEADS)).reshape(1, LANES)
    row = lambda b, c: b * nc + c
    full = lambda shape: pl.BlockSpec(shape, lambda b, c: (0,) * len(shape))
    n_pair = SSD_HEADS // 2
    y, conv_new, ssm_new = pl.pallas_call(
        functools.partial(_ssd_kernel, nc=nc, n_valid=n_valid),
        grid=(nb, nc),
        in_specs=[
            pl.BlockSpec((CHUNK, W_SSD), lambda b, c: (row(b, c), 0)),
            pl.BlockSpec((CHUNK, W_SSD), lambda b, c: (row(b, c), 1)),
            pl.BlockSpec((CHUNK, SSD_GN), lambda b, c: (row(b, c), 4)),
            pl.BlockSpec((CHUNK, SSD_GN), lambda b, c: (row(b, c), 5)),
            pl.BlockSpec((CHUNK, LANES), lambda b, c: (row(b, c), 0)),
            pl.BlockSpec((1, 8, SSD_CONV_CH), lambda b, c: (b, 0, 0)),
            pl.BlockSpec((1, n_pair, LANES, SSD_STATE), lambda b, c: (b, 0, 0, 0)),
            full((SSD_CONV, SSD_CONV_CH)),
            full((1, SSD_CONV_CH)),
            full((1, LANES)),
            full((1, LANES)),
            full((1, LANES)),
            full((1, W_SSD)),
        ],
        out_specs=[
            pl.BlockSpec((CHUNK, W_SSD), lambda b, c: (row(b, c), 0)),
            pl.BlockSpec((1, SSD_CONV - 1, SSD_CONV_CH), lambda b, c: (b, 0, 0)),
            pl.BlockSpec((1, n_pair, LANES, SSD_STATE), lambda b, c: (b, 0, 0, 0)),
        ],
        out_shape=[
            jax.ShapeDtypeStruct((rows, W_SSD), BF16),
            jax.ShapeDtypeStruct((nb, SSD_CONV - 1, SSD_CONV_CH), F32),
            jax.ShapeDtypeStruct((nb, n_pair, LANES, SSD_STATE), F32),
        ],
        scratch_shapes=[pltpu.VMEM((CHUNK + 8, SSD_CONV_CH), F32), pltpu.VMEM((n_pair, LANES, SSD_STATE), F32)],
        compiler_params=_params(("parallel", "arbitrary")),
        name="ssd_mixer",
    )(proj, proj, proj, proj, dt_raw, conv_state8, ssm_state.reshape(nb, n_pair, LANES, SSD_STATE),
      conv_w, conv_b.reshape(1, -1), pad(dt_bias), pad(a_log), pad(d_skip), norm_w.reshape(1, -1))
    return y, conv_new, ssm_new.reshape(nb, SSD_HEADS, 64, SSD_STATE)


def _pool_kernel(u_ref, buf_ref, pw_ref, ps_ref, y_ref, bnew_ref, ext_ref, *, nt, tl, n_valid, start):
    t = pl.program_id(1)
    H = POOL_BUF + 1

    @pl.when(t == 0)
    def _():
        ext_ref[0:H, :] = buf_ref[0]

    ext_ref[H:H + tl, :] = u_ref[...]
    pos = start + t * tl + lax.broadcasted_iota(jnp.int32, (tl, 1), 0)
    outs = []
    for g, w in enumerate(POOL_WINDOWS):
        cols = slice(g * POOL_GC, (g + 1) * POOL_GC)
        u = ext_ref[H:H + tl, cols]
        ssum = u
        for k in range(1, w):
            ssum = ssum + ext_ref[H - k:H - k + tl, cols]
        cnt = jnp.minimum(pos + 1, w).astype(F32)
        d = ssum / cnt - u
        outs.append(_dot(d, pw_ref[g]))
    y_ref[...] = (jnp.concatenate(outs, axis=1) * ps_ref[...]).astype(y_ref.dtype)

    @pl.when(t == nt - 1)
    def _():
        bnew_ref[0] = ext_ref[1 + n_valid:H + n_valid, :]

    ext_ref[0:H, :] = ext_ref[tl:tl + H, :]


def pool_mixer(proj, buf16, pool_w, pool_scale, *, nb, nt, tl, n_valid, start):
    assert n_valid == tl or nt == 1
    rows = nb * nt * tl
    ublk = (3 * 1024) // W_POOL
    return pl.pallas_call(
        functools.partial(_pool_kernel, nt=nt, tl=tl, n_valid=n_valid, start=start),
        grid=(nb, nt),
        in_specs=[
            pl.BlockSpec((tl, W_POOL), lambda b, t: (b * nt + t, ublk)),
            pl.BlockSpec((1, POOL_BUF + 1, W_POOL), lambda b, t: (b, 0, 0)),
            pl.BlockSpec((len(POOL_WINDOWS), POOL_GC, POOL_GC), lambda b, t: (0, 0, 0)),
            pl.BlockSpec((1, W_POOL), lambda b, t: (0, 0)),
        ],
        out_specs=[
            pl.BlockSpec((tl, W_POOL), lambda b, t: (b * nt + t, 0)),
            pl.BlockSpec((1, POOL_BUF, W_POOL), lambda b, t: (b, 0, 0)),
        ],
        out_shape=[
            jax.ShapeDtypeStruct((rows, W_POOL), BF16),
            jax.ShapeDtypeStruct((nb, POOL_BUF, W_POOL), F32),
        ],
        scratch_shapes=[pltpu.VMEM((tl + POOL_BUF + 1, W_POOL), F32)],
        compiler_params=_params(("parallel", "arbitrary")),
        name="pool_mixer",
    )(proj, buf16, pool_w.astype(BF16), pool_scale.reshape(1, -1))


def _ret_kernel(q_ref, k_ref, v_ref, g_ref, cos_ref, sin_ref, s0_ref, gnw_ref, y_ref, snew_ref, s_ref, *, nc, n_valid):
    c = pl.program_id(1)
    C = CHUNK

    @pl.when(c == 0)
    def _():
        s_ref[...] = s0_ref[0]

    cos_f = cos_ref[...]
    sin_f = sin_ref[...]
    ri = lax.broadcasted_iota(jnp.int32, (C, C), 0)
    ci = lax.broadcasted_iota(jnp.int32, (C, C), 1)
    causal = ri >= ci
    diff = jnp.where(causal, ri - ci, 0).astype(F32)
    rowi = lax.broadcasted_iota(jnp.int32, (C, 1), 0)
    rowf = rowi.astype(F32)
    for h in range(RET_HEADS):
        lg = math.log1p(-(2.0 ** (-5 - h)))
        sl = slice(h * HEAD_DIM, (h + 1) * HEAD_DIM)
        qh, kh, vh = q_ref[:, sl], k_ref[:, sl], v_ref[:, sl]
        qr = (qh * cos_f + pltpu.roll(qh, HEAD_DIM // 2, 1) * sin_f) * (HEAD_DIM ** -0.5)
        kr = kh * cos_f + pltpu.roll(kh, HEAD_DIM // 2, 1) * sin_f
        intra = jnp.where(causal, jnp.exp(lg * diff), 0.0)
        s = _dot_nt(qr, kr) * intra
        q_dec = jnp.exp(lg * (rowf + 1.0))
        k_dec = jnp.where(rowi < n_valid, jnp.exp(lg * (n_valid - 1.0 - rowf)), 0.0)
        sh = s_ref[h]
        y = _dot(s, vh) + _dot(qr * q_dec, sh)
        s_ref[h] = sh * math.exp(lg * n_valid) + _dot((kr * k_dec).T, vh)
        mu = jnp.mean(y, axis=-1, keepdims=True)
        yc = y - mu
        var = jnp.mean(yc * yc, axis=-1, keepdims=True)
        yn = yc * lax.rsqrt(var + NORM_EPS) * gnw_ref[:, sl]
        y_ref[:, sl] = (_silu(g_ref[:, sl]) * yn).astype(y_ref.dtype)

    @pl.when(c == nc - 1)
    def _():
        snew_ref[0] = s_ref[...]


def _rope_tables(start, n):
    half = HEAD_DIM // 2
    inv = ROPE_BASE ** (-jnp.arange(half, dtype=F32) / half)
    ang = (start + jnp.arange(n)).astype(F32)[:, None] * inv[None, :]
    cos, sin = jnp.cos(ang), jnp.sin(ang)
    return jnp.concatenate([cos, cos], axis=1), jnp.concatenate([-sin, sin], axis=1)


def retention_mixer(proj, ret_state, gn_w, *, nb, nc, n_valid, start):
    assert n_valid == CHUNK or nc == 1
    rows = nb * nc * CHUNK
    cos_f, sin_f = _rope_tables(start, nc * CHUNK)
    col = lambda j: pl.BlockSpec((CHUNK, W_RET), lambda b, c: (b * nc + c, j))
    return pl.pallas_call(
        functools.partial(_ret_kernel, nc=nc, n_valid=n_valid),
        grid=(nb, nc),
        in_specs=[
            col(4), col(5), col(6), col(7),
            pl.BlockSpec((CHUNK, HEAD_DIM), lambda b, c: (c, 0)),
            pl.BlockSpec((CHUNK, HEAD_DIM), lambda b, c: (c, 0)),
            pl.BlockSpec((1, RET_HEADS, HEAD_DIM, HEAD_DIM), lambda b, c: (b, 0, 0, 0)),
            pl.BlockSpec((1, W_RET), lambda b, c: (0, 0)),
        ],
        out_specs=[
            pl.BlockSpec((CHUNK, W_RET), lambda b, c: (b * nc + c, 0)),
            pl.BlockSpec((1, RET_HEADS, HEAD_DIM, HEAD_DIM), lambda b, c: (b, 0, 0, 0)),
        ],
        out_shape=[
            jax.ShapeDtypeStruct((rows, W_RET), BF16),
            jax.ShapeDtypeStruct((nb, RET_HEADS, HEAD_DIM, HEAD_DIM), F32),
        ],
        scratch_shapes=[pltpu.VMEM((RET_HEADS, HEAD_DIM, HEAD_DIM), F32)],
        compiler_params=_params(("parallel", "arbitrary")),
        name="retention_mixer",
    )(proj, proj, proj, proj, cos_f, sin_f, ret_state, gn_w.reshape(1, -1))


def _sb_prep_kernel(q_ref, k_ref, v_ref, qw_ref, kw_ref, qn_ref, kn_ref, kb_ref, vf_ref, vb_ref):
    for h in range(SB_HEADS):
        sl = slice(h * HEAD_DIM, (h + 1) * HEAD_DIM)
        q = q_ref[:, sl]
        k = k_ref[:, sl]
        qn = q * lax.rsqrt(jnp.mean(q * q, axis=-1, keepdims=True) + NORM_EPS) * qw_ref[...]
        kn = k * lax.rsqrt(jnp.mean(k * k, axis=-1, keepdims=True) + NORM_EPS) * kw_ref[...]
        qn_ref[:, sl] = qn.astype(BF16)
        kn_ref[:, sl] = kn
        kb_ref[:, sl] = kn.astype(BF16)
    v = v_ref[...]
    vf_ref[...] = v
    vb_ref[...] = v.astype(BF16)


def sb_prep(proj, q_norm, k_norm):
    m = proj.shape[0]
    tr = min(m, 256)
    col = lambda j: pl.BlockSpec((tr, W_SB), lambda i: (i, j))
    out = pl.BlockSpec((tr, W_SB), lambda i: (i, 0))
    sds = lambda dt: jax.ShapeDtypeStruct((m, W_SB), dt)
    return pl.pallas_call(
        _sb_prep_kernel,
        grid=(m // tr,),
        in_specs=[col(8), col(9), col(10), pl.BlockSpec((1, HEAD_DIM), lambda i: (0, 0)),
                  pl.BlockSpec((1, HEAD_DIM), lambda i: (0, 0))],
        out_specs=[out] * 5,
        out_shape=[sds(BF16), sds(F32), sds(BF16), sds(F32), sds(BF16)],
        compiler_params=_params(("parallel",)),
        name="sb_prep",
    )(proj, proj, proj, q_norm.reshape(1, -1), k_norm.reshape(1, -1))


def _log_sigmoid_pair(zz):
    lb = jnp.minimum(zz, 0.0) - jnp.log1p(jnp.exp(-jnp.abs(zz)))
    return lb, lb - zz


def _suffix_sum(l1, upper):
    h1 = l1.astype(BF16)
    r1 = l1 - h1.astype(F32)
    h2 = r1.astype(BF16)
    h3 = (r1 - h2.astype(F32)).astype(BF16)
    dot = lambda a: jnp.dot(a, upper, preferred_element_type=F32)
    return dot(h1) + dot(h2) + dot(h3)


def _sb_prompt_kernel(q_ref, k_ref, v_ref, o_ref, *, tile):
    qi = pl.program_id(2)
    T = tile
    q = q_ref[...]
    ri = lax.broadcasted_iota(jnp.int32, (T, T), 0)
    ci = lax.broadcasted_iota(jnp.int32, (T, T), 1)
    upper = (ri > ci).astype(BF16)
    scale = HEAD_DIM ** -0.5

    def tile_step(kblk, carry, masked):
        acc, run = carry
        off = pl.multiple_of(kblk * T, T)
        k = k_ref[pl.ds(off, T), :]
        v = v_ref[pl.ds(off, T), :]
        zz = lax.dot_general(q, k, _NT, preferred_element_type=F32) * scale
        lb, l1 = _log_sigmoid_pair(zz)
        if masked:
            allowed = ci < ri
            l1 = jnp.where(allowed, l1, 0.0)
        suf = _suffix_sum(l1, upper)
        a = jnp.exp(lb + suf + run)
        if masked:
            a = jnp.where(allowed, a, 0.0)
        acc = acc + jnp.dot(a.astype(BF16), v, preferred_element_type=F32)
        return acc, run + suf[:, 0:1] + l1[:, 0:1]

    carry = tile_step(qi, (jnp.zeros((T, HEAD_DIM), F32), jnp.zeros((T, 1), F32)), True)
    carry = lax.fori_loop(0, qi, lambda t, cr: tile_step(qi - 1 - t, cr, False), carry)
    o_ref[...] = carry[0].astype(o_ref.dtype)


def sb_prompt_attention(qn, kb, vb, *, nb, seq, tile=128):
    nq = seq // tile
    return pl.pallas_call(
        functools.partial(_sb_prompt_kernel, tile=tile),
        grid=(nb, SB_HEADS, nq),
        in_specs=[
            pl.BlockSpec((tile, HEAD_DIM), lambda b, h, i: (b * nq + i, h)),
            pl.BlockSpec((seq, HEAD_DIM), lambda b, h, i: (b, h)),
            pl.BlockSpec((seq, HEAD_DIM), lambda b, h, i: (b, h)),
        ],
        out_specs=pl.BlockSpec((tile, HEAD_DIM), lambda b, h, i: (b * nq + i, h)),
        out_shape=jax.ShapeDtypeStruct((nb * seq, W_SB), BF16),
        compiler_params=_params(("parallel", "parallel", "arbitrary")),
        name="sb_prompt_attention",
    )(qn, kb, vb)


def _sb_decode_kernel(pt_ref, q_ref, k_ref, v_ref, o_ref, acc_ref, run_ref, qbd_ref, *, n_pages):
    s = pl.program_id(1)
    head_of_lane = jnp.right_shift(lax.broadcasted_iota(jnp.int32, (SB_HEADS, W_SB), 1), 7)
    own = head_of_lane == lax.broadcasted_iota(jnp.int32, (SB_HEADS, W_SB), 0)

    @pl.when(s == 0)
    def _():
        acc_ref[...] = jnp.zeros_like(acc_ref)
        run_ref[...] = jnp.zeros_like(run_ref)
        qbd_ref[...] = jnp.where(own, jnp.broadcast_to(q_ref[0].astype(F32), (SB_HEADS, W_SB)), 0.0).astype(BF16)

    P = PAGE_SIZE
    ri = lax.broadcasted_iota(jnp.int32, (P, P), 0)
    ci = lax.broadcasted_iota(jnp.int32, (P, P), 1)
    upper = (ri > ci).astype(BF16)
    zz = lax.dot_general(qbd_ref[...], k_ref[0].astype(BF16), _NT, preferred_element_type=F32) * (HEAD_DIM ** -0.5)
    lb, l1 = _log_sigmoid_pair(zz)
    suf = _suffix_sum(l1, upper)
    a = jnp.exp(lb + suf + run_ref[...])
    acc_ref[...] += jnp.dot(a.astype(BF16), v_ref[0].astype(BF16), preferred_element_type=F32)
    run_ref[...] += suf[:, 0:1] + l1[:, 0:1]

    @pl.when(s == n_pages - 1)
    def _():
        o_ref[0] = jnp.sum(jnp.where(own, acc_ref[...], 0.0), axis=0, keepdims=True).astype(o_ref.dtype)


def sb_decode_attention(qn, cache_k, cache_v, pages):
    nb, n_pages = pages.shape
    page = lambda b, s, pt: (pt[b, n_pages - 1 - s], 0, 0)
    grid_spec = pltpu.PrefetchScalarGridSpec(
        num_scalar_prefetch=1,
        grid=(nb, n_pages),
        in_specs=[
            pl.BlockSpec((1, 1, W_SB), lambda b, s, pt: (b, 0, 0)),
            pl.BlockSpec((1, PAGE_SIZE, W_SB), page),
            pl.BlockSpec((1, PAGE_SIZE, W_SB), page),
        ],
        out_specs=pl.BlockSpec((1, 1, W_SB), lambda b, s, pt: (b, 0, 0)),
        scratch_shapes=[pltpu.VMEM((SB_HEADS, W_SB), F32), pltpu.VMEM((SB_HEADS, 1), F32),
                        pltpu.VMEM((SB_HEADS, W_SB), BF16)],
    )
    out = pl.pallas_call(
        functools.partial(_sb_decode_kernel, n_pages=n_pages),
        grid_spec=grid_spec,
        out_shape=jax.ShapeDtypeStruct((nb, 1, W_SB), BF16),
        compiler_params=_params(("parallel", "arbitrary")),
        name="sb_decode_attention",
    )(pages, qn.reshape(nb, 1, W_SB), cache_k, cache_v)
    return out.reshape(nb, W_SB)


def _dense_in(x, norm_w, w_main, w_dt):
    h = rmsnorm_bf16(x, norm_w)
    proj = matmul(h, w_main, tm=512, tn=1024, tk=D_MODEL, name="in_proj")
    dt_raw = matmul(h, w_dt, tm=1024, tn=LANES, tk=D_MODEL, name="dt_proj")
    return proj, dt_raw


def _dense_out(x, mix, w_out, norm2_w, w_ff1, w_ff2):
    x = matmul(mix, w_out, tm=512, tn=1024, tk=D_MODEL, epilogue="residual", residual=x, name="out_proj")
    h2 = rmsnorm_bf16(x, norm2_w)
    ff = matmul(h2, w_ff1, tm=512, tn=1024, tk=D_MODEL, out_dtype=BF16, epilogue="relu2", name="ffn_up")
    return matmul(ff, w_ff2, tm=1024, tn=1024, tk=2048, epilogue="residual", residual=x, name="ffn_down")


def kernel(x_prompt, x_sample, cache_k, cache_v, state_ssm, state_conv, state_pool, state_ret, page_table, norm1_w, w_in, conv_w, conv_b, dt_bias, a_log, d_skip, ssd_norm_w, pool_w, pool_scale, ret_gn_w, sb_q_norm, sb_k_norm, w_out, norm2_w, w_ff1, w_ff2):
    bp, seq, _ = x_prompt.shape
    bs = x_sample.shape[0]
    n_phys = cache_k.shape[1]
    past_len = page_table.shape[1] * PAGE_SIZE
    ncp = seq // CHUNK
    xp = x_prompt.reshape(bp * seq, D_MODEL)
    xs = x_sample.reshape(bs, D_MODEL)
    ck = cache_k.reshape(DEPTH * n_phys, PAGE_SIZE, W_SB)
    cv = cache_v.reshape(DEPTH * n_phys, PAGE_SIZE, W_SB)
    zeros = lambda *shape: jnp.zeros(shape, F32)
    outs = {name: [] for name in ("kp", "vp", "ks", "vs", "sp", "ss", "cp", "cs", "pp", "ps", "rp", "rs")}

    for l in range(DEPTH):
        w_main = jnp.concatenate([w_in[l, :, :DT_OFF], w_in[l, :, DT_OFF + SSD_HEADS:]], axis=1).astype(BF16)
        w_dt = jnp.pad(w_in[l, :, DT_OFF:DT_OFF + SSD_HEADS], ((0, 0), (0, LANES - SSD_HEADS))).astype(BF16)
        w_o, w_1, w_2 = w_out[l].astype(BF16), w_ff1[l].astype(BF16), w_ff2[l].astype(BF16)
        ssd_w = (conv_w[l], conv_b[l], dt_bias[l], a_log[l], d_skip[l], ssd_norm_w[l])

        proj, dt_raw = _dense_in(xp, norm1_w[l], w_main, w_dt)
        y_ssd, cp, sp = ssd_mixer(proj, dt_raw, zeros(bp, 8, SSD_CONV_CH), zeros(bp, SSD_HEADS, 64, SSD_STATE), *ssd_w,
                                  nb=bp, nc=ncp, n_valid=CHUNK)
        y_pool, pp = pool_mixer(proj, zeros(bp, POOL_BUF + 1, W_POOL), pool_w[l], pool_scale[l],
                                nb=bp, nt=seq // 256, tl=256, n_valid=256, start=0)
        y_ret, rp = retention_mixer(proj, zeros(bp, RET_HEADS, HEAD_DIM, HEAD_DIM), ret_gn_w[l],
                                    nb=bp, nc=ncp, n_valid=CHUNK, start=0)
        qn, kn, kb, vf, vb = sb_prep(proj, sb_q_norm[l], sb_k_norm[l])
        y_sb = sb_prompt_attention(qn, kb, vb, nb=bp, seq=seq)
        mix = jnp.concatenate([y_ssd, y_pool, y_ret, y_sb], axis=1)
        xp = _dense_out(xp, mix, w_o, norm2_w[l], w_1, w_2)
        outs["kp"].append(kn.reshape(bp, seq, SB_HEADS, HEAD_DIM))
        outs["vp"].append(vf.reshape(bp, seq, SB_HEADS, HEAD_DIM))
        outs["sp"].append(sp); outs["cp"].append(cp); outs["pp"].append(pp); outs["rp"].append(rp)

        proj_s, dt_s = _dense_in(xs, norm1_w[l], w_main, w_dt)
        padded = lambda a: jnp.pad(a[:, None, :], ((0, 0), (0, CHUNK - 1), (0, 0))).reshape(bs * CHUNK, a.shape[1])
        proj_sp, dt_sp = padded(proj_s), padded(dt_s)
        first = lambda y: y.reshape(bs, CHUNK, -1)[:, 0, :]
        y_ssd, cs, ss = ssd_mixer(proj_sp, dt_sp, jnp.pad(state_conv[l], ((0, 0), (5, 0), (0, 0))), state_ssm[l], *ssd_w,
                                  nb=bs, nc=1, n_valid=1)
        y_pool, ps = pool_mixer(proj_sp, jnp.pad(state_pool[l], ((0, 0), (1, 0), (0, 0))), pool_w[l], pool_scale[l],
                                nb=bs, nt=1, tl=CHUNK, n_valid=1, start=past_len)
        y_ret, rs = retention_mixer(proj_sp, state_ret[l], ret_gn_w[l], nb=bs, nc=1, n_valid=1, start=past_len)
        qn, kn, kb, vf, vb = sb_prep(proj_s, sb_q_norm[l], sb_k_norm[l])
        y_sb = sb_decode_attention(qn, ck, cv, page_table + l * n_phys)
        mix = jnp.concatenate([first(y_ssd), first(y_pool), first(y_ret), y_sb], axis=1)
        xs = _dense_out(xs, mix, w_o, norm2_w[l], w_1, w_2)
        outs["ks"].append(kn.reshape(bs, 1, SB_HEADS, HEAD_DIM))
        outs["vs"].append(vf.reshape(bs, 1, SB_HEADS, HEAD_DIM))
        outs["ss"].append(ss); outs["cs"].append(cs); outs["ps"].append(ps); outs["rs"].append(rs)

    st = lambda name: jnp.stack(outs[name])
    return (xp.reshape(bp, seq, D_MODEL), xs.reshape(bs, 1, D_MODEL), st("kp"), st("vp"), st("ks"), st("vs"),
            st("sp"), st("ss"), st("cp"), st("cs"), st("pp"), st("ps"), st("rp"), st("rs"))
```

```python
import functools
import math

import jax
import jax.numpy as jnp
from jax import lax
from jax.experimental import pallas as pl
from jax.experimental.pallas import tpu as pltpu

F32 = jnp.float32
BF16 = jnp.bfloat16

D_MODEL = 4096
DEPTH = 4
PAGE_SIZE = 128
W_SSD = W_POOL = W_RET = W_SB = 1024
SSD_HEADS = 16
SSD_GROUPS = 4
SSD_STATE = 128
SSD_GN = SSD_GROUPS * SSD_STATE
SSD_CONV = 4
SSD_CONV_CH = W_SSD + 2 * SSD_GN
POOL_WINDOWS = (2, 4, 8, 16)
POOL_GC = W_POOL // len(POOL_WINDOWS)
POOL_BUF = max(POOL_WINDOWS) - 1
RET_HEADS = 8
HEAD_DIM = 128
SB_HEADS = 8
ROPE_BASE = 10000.0
D_FF = 4 * D_MODEL
NORM_EPS = 1e-6
CHUNK = 128
DECODE_PAGES_PER_STEP = 4
PROJ_A = W_SSD * 2 + SSD_GN * 2
PROJ_B_COLS = ("u", "rq", "rk", "rv", "rg", "sq", "sk", "sv")
PROJ_B = 1024 * len(PROJ_B_COLS)
LANES = 128
NEG_BIG = -1e30
VMEM_LIMIT = 48 * 1024 * 1024

_NT = (((1,), (1,)), ((), ()))


def _params(sem):
    return pltpu.CompilerParams(dimension_semantics=sem, vmem_limit_bytes=VMEM_LIMIT)


def _silu(x):
    return x / (1.0 + jnp.exp(-x))


def _softplus(x):
    return jnp.maximum(x, 0.0) + jnp.log1p(jnp.exp(-jnp.abs(x)))


def _dot(a, b):
    return jnp.dot(a.astype(BF16), b.astype(BF16), preferred_element_type=F32)


def _dot_nt(a, b):
    return lax.dot_general(a.astype(BF16), b.astype(BF16), _NT, preferred_element_type=F32)


def _rmsnorm_kernel(x_ref, w_ref, o_ref):
    x = x_ref[...]
    ms = jnp.mean(x * x, axis=-1, keepdims=True)
    o_ref[...] = (x * lax.rsqrt(ms + NORM_EPS) * w_ref[...]).astype(o_ref.dtype)


def rmsnorm_bf16(x, w):
    m, d = x.shape
    tr = min(m, 256)
    return pl.pallas_call(
        _rmsnorm_kernel,
        grid=(m // tr,),
        in_specs=[pl.BlockSpec((tr, d), lambda i: (i, 0)), pl.BlockSpec((1, d), lambda i: (0, 0))],
        out_specs=pl.BlockSpec((tr, d), lambda i: (i, 0)),
        out_shape=jax.ShapeDtypeStruct((m, d), BF16),
        compiler_params=_params(("parallel",)),
        name="rmsnorm",
    )(x, w.reshape(1, d))


def _mm_kernel(*refs, nk, epilogue):
    xb_ref, xs_ref, w_ref = refs[:3]
    n_in = 5 if epilogue == "residual" else 3
    rb_ref, rs_ref = refs[3:5] if epilogue == "residual" else (None, None)
    ob_ref, os_ref = refs[n_in:n_in + 2]
    accb_ref, accs_ref = refs[n_in + 2:] if nk > 1 else (None, None)
    k = pl.program_id(2)

    def finish(acc, r_ref, o_ref):
        if epilogue == "relu2":
            r = jnp.maximum(acc, 0.0)
            acc = r * r
        elif epilogue == "residual":
            acc = acc + r_ref[...]
        o_ref[...] = acc.astype(o_ref.dtype)

    def run(x_ref, r_ref, o_ref, acc_ref):
        part = jnp.dot(x_ref[...], w_ref[...], preferred_element_type=F32)
        if nk == 1:
            finish(part, r_ref, o_ref)
            return

        @pl.when(k == 0)
        def _():
            acc_ref[...] = part

        @pl.when(k > 0)
        def _():
            acc_ref[...] += part

        @pl.when(k == nk - 1)
        def _():
            finish(acc_ref[...], r_ref, o_ref)

    run(xb_ref, rb_ref, ob_ref, accb_ref)

    @pl.when(pl.program_id(1) == 0)
    def _():
        run(xs_ref, rs_ref, os_ref, accs_ref)


def matmul(xb, xs, w, layer, *, tm, tn, tk, out_dtype=F32, epilogue=None, residual=None, name="matmul"):
    m, kdim = xb.shape
    ms = xs.shape[0]
    n = w.shape[2]
    tm, tn, tk = min(tm, m), min(tn, n), min(tk, kdim)
    nk = kdim // tk
    in_specs = [
        pl.BlockSpec((tm, tk), lambda j, i, k: (i, k)),
        pl.BlockSpec((ms, tk), lambda j, i, k: (0, k)),
        pl.BlockSpec((None, tk, tn), lambda j, i, k: (layer, k, j)),
    ]
    args = [xb, xs, w]
    if epilogue == "residual":
        in_specs += [pl.BlockSpec((tm, tn), lambda j, i, k: (i, j)), pl.BlockSpec((ms, tn), lambda j, i, k: (0, j))]
        args += list(residual)
    return pl.pallas_call(
        functools.partial(_mm_kernel, nk=nk, epilogue=epilogue),
        grid=(n // tn, m // tm, nk),
        in_specs=in_specs,
        out_specs=[pl.BlockSpec((tm, tn), lambda j, i, k: (i, j)), pl.BlockSpec((ms, tn), lambda j, i, k: (0, j))],
        out_shape=[jax.ShapeDtypeStruct((m, n), out_dtype), jax.ShapeDtypeStruct((ms, n), out_dtype)],
        scratch_shapes=[] if nk == 1 else [pltpu.VMEM((tm, tn), F32), pltpu.VMEM((ms, tn), F32)],
        compiler_params=_params(("parallel", "arbitrary", "arbitrary")),
        name=name,
    )(*args)


def _ssd_kernel(xs_ref, z_ref, bm_ref, cm_ref, dt_ref, cs_ref, s0_ref, cw_ref, cb_ref, dtb_ref, alog_ref,
                dsk_ref, nw_ref, y_ref, cnew_ref, snew_ref, ext_ref, s_ref, *, nc, n_valid):
    c = pl.program_id(1)
    C = CHUNK

    @pl.when(c == 0)
    def _():
        ext_ref[0:8, :] = cs_ref[0]
        s_ref[...] = s0_ref[0]

    ext_ref[8:8 + C, 0:W_SSD] = xs_ref[...]
    ext_ref[8:8 + C, W_SSD:W_SSD + SSD_GN] = bm_ref[...]
    ext_ref[8:8 + C, W_SSD + SSD_GN:] = cm_ref[...]
    cw = cw_ref[...]
    conv = cb_ref[...] + ext_ref[8:8 + C, :] * cw[3:4, :]
    for tap in range(SSD_CONV - 1):
        conv = conv + ext_ref[5 + tap:5 + tap + C, :] * cw[tap:tap + 1, :]

    @pl.when(c == nc - 1)
    def _():
        cnew_ref[0] = ext_ref[5 + n_valid:8 + n_valid, :]

    ext_ref[0:8, :] = ext_ref[C:C + 8, :]

    xbc = _silu(conv)
    x = xbc[:, :W_SSD]
    bmat = xbc[:, W_SSD:W_SSD + SSD_GN]
    cmat = xbc[:, W_SSD + SSD_GN:]

    dt = _softplus(dt_ref[...] + dtb_ref[...])
    if n_valid < C:
        dt = jnp.where(lax.broadcasted_iota(jnp.int32, (C, LANES), 0) < n_valid, dt, 0.0)
    da = dt * (-jnp.exp(alog_ref[...]))
    ri = lax.broadcasted_iota(jnp.int32, (C, C), 0)
    ci = lax.broadcasted_iota(jnp.int32, (C, C), 1)
    tril = ri >= ci
    cum = jnp.dot(tril.astype(F32), da, precision=lax.Precision.HIGHEST, preferred_element_type=F32)
    cum_t = cum.T
    end = cum[C - 1:C, :]
    e_end = jnp.exp(end)
    lo = lax.broadcasted_iota(jnp.int32, (C, LANES), 1) < 64
    row_lo = lax.broadcasted_iota(jnp.int32, (C, LANES), 0) < 64
    dsk = dsk_ref[...]

    def pick(v, h0):
        return jnp.where(lo, v[:, h0:h0 + 1], v[:, h0 + 1:h0 + 2])

    ys = []
    for g in range(SSD_GROUPS):
        bg = bmat[:, g * SSD_STATE:(g + 1) * SSD_STATE].astype(BF16)
        cg = cmat[:, g * SSD_STATE:(g + 1) * SSD_STATE].astype(BF16)
        cb = _dot_nt(cg, bg)
        for pr in range(2):
            p = 2 * g + pr
            h0 = 2 * p
            xp = x[:, p * LANES:(p + 1) * LANES]
            xdt = xp * pick(dt, h0)
            y_in = jnp.zeros((C, LANES), F32)
            for h, keep in ((h0, lo), (h0 + 1, jnp.logical_not(lo))):
                seg = cum[:, h:h + 1] - cum_t[h:h + 1, :]
                decay = jnp.exp(jnp.where(tril, seg, NEG_BIG))
                y_in = y_in + _dot(cb * decay, jnp.where(keep, xdt, 0.0))
            sp = s_ref[p]
            y_off = _dot_nt(cg, sp) * jnp.exp(pick(cum, h0))
            ys.append(y_in + y_off + jnp.where(lo[0:1], dsk[:, h0:h0 + 1], dsk[:, h0 + 1:h0 + 2]) * xp)
            w_end = jnp.exp(pick(jnp.broadcast_to(end, (C, LANES)) - cum, h0))
            s_scale = jnp.where(row_lo, e_end[:, h0:h0 + 1], e_end[:, h0 + 1:h0 + 2])
            s_ref[p] = sp * s_scale + _dot((xdt * w_end).T, bg)

    y = jnp.concatenate(ys, axis=1) * _silu(z_ref[...])
    ms = jnp.mean(y * y, axis=-1, keepdims=True)
    y_ref[...] = (y * lax.rsqrt(ms + NORM_EPS) * nw_ref[...]).astype(y_ref.dtype)

    @pl.when(c == nc - 1)
    def _():
        snew_ref[0] = s_ref[...]


def ssd_mixer(proj, dt_raw, conv_state8, ssm_state, conv_w, conv_b, dt_bias, a_log, d_skip, norm_w, *, nb, nc, n_valid):
    assert n_valid == CHUNK or nc == 1
    rows = nb * nc * CHUNK
    pad = lambda v: jnp.pad(v, (0, LANES - SSD_HEADS)).reshape(1, LANES)
    row = lambda b, c: b * nc + c
    full = lambda shape: pl.BlockSpec(shape, lambda b, c: (0,) * len(shape))
    n_pair = SSD_HEADS // 2
    y, conv_new, ssm_new = pl.pallas_call(
        functools.partial(_ssd_kernel, nc=nc, n_valid=n_valid),
        grid=(nb, nc),
        in_specs=[
            pl.BlockSpec((CHUNK, W_SSD), lambda b, c: (row(b, c), 0)),
            pl.BlockSpec((CHUNK, W_SSD), lambda b, c: (row(b, c), 1)),
            pl.BlockSpec((CHUNK, SSD_GN), lambda b, c: (row(b, c), 4)),
            pl.BlockSpec((CHUNK, SSD_GN), lambda b, c: (row(b, c), 5)),
            pl.BlockSpec((CHUNK, LANES), lambda b, c: (row(b, c), 0)),
            pl.BlockSpec((1, 8, SSD_CONV_CH), lambda b, c: (b, 0, 0)),
            pl.BlockSpec((1, n_pair, LANES, SSD_STATE), lambda b, c: (b, 0, 0, 0)),
            full((SSD_CONV, SSD_CONV_CH)),
            full((1, SSD_CONV_CH)),
            full((1, LANES)),
            full((1, LANES)),
            full((1, LANES)),
            full((1, W_SSD)),
        ],
        out_specs=[
            pl.BlockSpec((CHUNK, W_SSD), lambda b, c: (row(b, c), 0)),
            pl.BlockSpec((1, SSD_CONV - 1, SSD_CONV_CH), lambda b, c: (b, 0, 0)),
            pl.BlockSpec((1, n_pair, LANES, SSD_STATE), lambda b, c: (b, 0, 0, 0)),
        ],
        out_shape=[
            jax.ShapeDtypeStruct((rows, W_SSD), BF16),
            jax.ShapeDtypeStruct((nb, SSD_CONV - 1, SSD_CONV_CH), F32),
            jax.ShapeDtypeStruct((nb, n_pair, LANES, SSD_STATE), F32),
        ],
        scratch_shapes=[pltpu.VMEM((CHUNK + 8, SSD_CONV_CH), F32), pltpu.VMEM((n_pair, LANES, SSD_STATE), F32)],
        compiler_params=_params(("parallel", "arbitrary")),
        name="ssd_mixer",
    )(proj, proj, proj, proj, dt_raw, conv_state8, ssm_state.reshape(nb, n_pair, LANES, SSD_STATE),
      conv_w, conv_b.reshape(1, -1), pad(dt_bias), pad(a_log), pad(d_skip), norm_w.reshape(1, -1))
    return y, conv_new, ssm_new.reshape(nb, SSD_HEADS, 64, SSD_STATE)


def _pool_kernel(u_ref, buf_ref, pw_ref, ps_ref, y_ref, bnew_ref, ext_ref, *, nt, tl, n_valid, start):
    t = pl.program_id(1)
    H = POOL_BUF + 1

    @pl.when(t == 0)
    def _():
        ext_ref[0:H, :] = buf_ref[0]

    ext_ref[H:H + tl, :] = u_ref[...]
    pos = start + t * tl + lax.broadcasted_iota(jnp.int32, (tl, 1), 0)
    outs = []
    for g, w in enumerate(POOL_WINDOWS):
        cols = slice(g * POOL_GC, (g + 1) * POOL_GC)
        u = ext_ref[H:H + tl, cols]
        ssum = u
        for k in range(1, w):
            ssum = ssum + ext_ref[H - k:H - k + tl, cols]
        cnt = jnp.minimum(pos + 1, w).astype(F32)
        d = ssum / cnt - u
        outs.append(_dot(d, pw_ref[g]))
    y_ref[...] = (jnp.concatenate(outs, axis=1) * ps_ref[...]).astype(y_ref.dtype)

    @pl.when(t == nt - 1)
    def _():
        bnew_ref[0] = ext_ref[1 + n_valid:H + n_valid, :]

    ext_ref[0:H, :] = ext_ref[tl:tl + H, :]


def pool_mixer(proj, buf16, pool_w, pool_scale, *, nb, nt, tl, n_valid, start):
    assert n_valid == tl or nt == 1
    rows = nb * nt * tl
    ublk = PROJ_B_COLS.index("u")
    return pl.pallas_call(
        functools.partial(_pool_kernel, nt=nt, tl=tl, n_valid=n_valid, start=start),
        grid=(nb, nt),
        in_specs=[
            pl.BlockSpec((tl, W_POOL), lambda b, t: (b * nt + t, ublk)),
            pl.BlockSpec((1, POOL_BUF + 1, W_POOL), lambda b, t: (b, 0, 0)),
            pl.BlockSpec((len(POOL_WINDOWS), POOL_GC, POOL_GC), lambda b, t: (0, 0, 0)),
            pl.BlockSpec((1, W_POOL), lambda b, t: (0, 0)),
        ],
        out_specs=[
            pl.BlockSpec((tl, W_POOL), lambda b, t: (b * nt + t, 0)),
            pl.BlockSpec((1, POOL_BUF, W_POOL), lambda b, t: (b, 0, 0)),
        ],
        out_shape=[
            jax.ShapeDtypeStruct((rows, W_POOL), BF16),
            jax.ShapeDtypeStruct((nb, POOL_BUF, W_POOL), F32),
        ],
        scratch_shapes=[pltpu.VMEM((tl + POOL_BUF + 1, W_POOL), F32)],
        compiler_params=_params(("parallel", "arbitrary")),
        name="pool_mixer",
    )(proj, buf16, pool_w.astype(BF16), pool_scale.reshape(1, -1))


def _ret_kernel(q_ref, k_ref, v_ref, g_ref, cos_ref, sin_ref, s0_ref, gnw_ref, y_ref, snew_ref, s_ref, *, nc, n_valid):
    c = pl.program_id(1)
    C = CHUNK

    @pl.when(c == 0)
    def _():
        s_ref[...] = s0_ref[0]

    cos_f = cos_ref[...]
    sin_f = sin_ref[...]
    ri = lax.broadcasted_iota(jnp.int32, (C, C), 0)
    ci = lax.broadcasted_iota(jnp.int32, (C, C), 1)
    causal = ri >= ci
    diff = jnp.where(causal, ri - ci, 0).astype(F32)
    rowi = lax.broadcasted_iota(jnp.int32, (C, 1), 0)
    rowf = rowi.astype(F32)
    for h in range(RET_HEADS):
        lg = math.log1p(-(2.0 ** (-5 - h)))
        sl = slice(h * HEAD_DIM, (h + 1) * HEAD_DIM)
        qh, kh, vh = q_ref[:, sl], k_ref[:, sl], v_ref[:, sl]
        qr = (qh * cos_f + pltpu.roll(qh, HEAD_DIM // 2, 1) * sin_f) * (HEAD_DIM ** -0.5)
        kr = kh * cos_f + pltpu.roll(kh, HEAD_DIM // 2, 1) * sin_f
        intra = jnp.where(causal, jnp.exp(lg * diff), 0.0)
        s = _dot_nt(qr, kr) * intra
        q_dec = jnp.exp(lg * (rowf + 1.0))
        k_dec = jnp.where(rowi < n_valid, jnp.exp(lg * (n_valid - 1.0 - rowf)), 0.0)
        sh = s_ref[h]
        y = _dot(s, vh) + _dot(qr * q_dec, sh)
        s_ref[h] = sh * math.exp(lg * n_valid) + _dot((kr * k_dec).T, vh)
        mu = jnp.mean(y, axis=-1, keepdims=True)
        yc = y - mu
        var = jnp.mean(yc * yc, axis=-1, keepdims=True)
        yn = yc * lax.rsqrt(var + NORM_EPS) * gnw_ref[:, sl]
        y_ref[:, sl] = (_silu(g_ref[:, sl]) * yn).astype(y_ref.dtype)

    @pl.when(c == nc - 1)
    def _():
        snew_ref[0] = s_ref[...]


def _rope_tables(start, n):
    half = HEAD_DIM // 2
    inv = ROPE_BASE ** (-jnp.arange(half, dtype=F32) / half)
    ang = (start + jnp.arange(n)).astype(F32)[:, None] * inv[None, :]
    cos, sin = jnp.cos(ang), jnp.sin(ang)
    return jnp.concatenate([cos, cos], axis=1), jnp.concatenate([-sin, sin], axis=1)


def retention_mixer(proj, ret_state, gn_w, *, nb, nc, n_valid, start):
    assert n_valid == CHUNK or nc == 1
    rows = nb * nc * CHUNK
    cos_f, sin_f = _rope_tables(start, nc * CHUNK)
    col = lambda j: pl.BlockSpec((CHUNK, W_RET), lambda b, c: (b * nc + c, j))
    return pl.pallas_call(
        functools.partial(_ret_kernel, nc=nc, n_valid=n_valid),
        grid=(nb, nc),
        in_specs=[
            *[col(PROJ_B_COLS.index(name)) for name in ("rq", "rk", "rv", "rg")],
            pl.BlockSpec((CHUNK, HEAD_DIM), lambda b, c: (c, 0)),
            pl.BlockSpec((CHUNK, HEAD_DIM), lambda b, c: (c, 0)),
            pl.BlockSpec((1, RET_HEADS, HEAD_DIM, HEAD_DIM), lambda b, c: (b, 0, 0, 0)),
            pl.BlockSpec((1, W_RET), lambda b, c: (0, 0)),
        ],
        out_specs=[
            pl.BlockSpec((CHUNK, W_RET), lambda b, c: (b * nc + c, 0)),
            pl.BlockSpec((1, RET_HEADS, HEAD_DIM, HEAD_DIM), lambda b, c: (b, 0, 0, 0)),
        ],
        out_shape=[
            jax.ShapeDtypeStruct((rows, W_RET), BF16),
            jax.ShapeDtypeStruct((nb, RET_HEADS, HEAD_DIM, HEAD_DIM), F32),
        ],
        scratch_shapes=[pltpu.VMEM((RET_HEADS, HEAD_DIM, HEAD_DIM), F32)],
        compiler_params=_params(("parallel", "arbitrary")),
        name="retention_mixer",
    )(proj, proj, proj, proj, cos_f, sin_f, ret_state, gn_w.reshape(1, -1))


def _sb_prep_kernel(q_ref, k_ref, v_ref, qw_ref, kw_ref, qn_ref, kn_ref, kb_ref, vf_ref, vb_ref):
    for h in range(SB_HEADS):
        sl = slice(h * HEAD_DIM, (h + 1) * HEAD_DIM)
        q = q_ref[:, sl]
        k = k_ref[:, sl]
        qn = q * lax.rsqrt(jnp.mean(q * q, axis=-1, keepdims=True) + NORM_EPS) * qw_ref[...]
        kn = k * lax.rsqrt(jnp.mean(k * k, axis=-1, keepdims=True) + NORM_EPS) * kw_ref[...]
        qn_ref[:, sl] = qn.astype(BF16)
        kn_ref[:, sl] = kn
        kb_ref[:, sl] = kn.astype(BF16)
    v = v_ref[...]
    vf_ref[...] = v
    vb_ref[...] = v.astype(BF16)


def sb_prep(proj, q_norm, k_norm):
    m = proj.shape[0]
    tr = min(m, 256)
    col = lambda j: pl.BlockSpec((tr, W_SB), lambda i: (i, j))
    out = pl.BlockSpec((tr, W_SB), lambda i: (i, 0))
    sds = lambda dt: jax.ShapeDtypeStruct((m, W_SB), dt)
    return pl.pallas_call(
        _sb_prep_kernel,
        grid=(m // tr,),
        in_specs=[*[col(PROJ_B_COLS.index(name)) for name in ("sq", "sk", "sv")],
                  pl.BlockSpec((1, HEAD_DIM), lambda i: (0, 0)),
                  pl.BlockSpec((1, HEAD_DIM), lambda i: (0, 0))],
        out_specs=[out] * 5,
        out_shape=[sds(BF16), sds(F32), sds(BF16), sds(F32), sds(BF16)],
        compiler_params=_params(("parallel",)),
        name="sb_prep",
    )(proj, proj, proj, q_norm.reshape(1, -1), k_norm.reshape(1, -1))


def _log_sigmoid_pair(zz):
    lb = jnp.minimum(zz, 0.0) - jnp.log(1.0 + jnp.exp(-jnp.abs(zz)))
    return lb, lb - zz


def _suffix_matrix(t):
    row = lax.broadcasted_iota(jnp.int32, (t, 2 * t), 0)
    col = lax.broadcasted_iota(jnp.int32, (t, 2 * t), 1)
    return jnp.logical_or(col >= t, row > col).astype(BF16)


def _suffix_and_total(l1, suffix_matrix):
    t = l1.shape[1]
    hi = l1.astype(BF16)
    lo = (l1 - hi.astype(F32)).astype(BF16)
    both = (jnp.dot(hi, suffix_matrix, preferred_element_type=F32)
            + jnp.dot(lo, suffix_matrix, preferred_element_type=F32))
    return both[:, :t], both[:, t:]


def _sb_prompt_kernel(q_ref, k_ref, v_ref, o_ref, acc_ref, run_ref):
    qi = pl.program_id(1)
    T = CHUNK
    ri = lax.broadcasted_iota(jnp.int32, (T, T), 0)
    ci = lax.broadcasted_iota(jnp.int32, (T, T), 1)
    allowed = ci < ri
    sfx = _suffix_matrix(T)
    scale = HEAD_DIM ** -0.5

    def tile_step(kblk, diagonal):
        off = pl.multiple_of(kblk * T, T)
        heads = range(SB_HEADS)
        sl = lambda h: slice(h * HEAD_DIM, (h + 1) * HEAD_DIM)
        zz = [lax.dot_general(q_ref[:, sl(h)], k_ref[pl.ds(off, T), sl(h)], _NT, preferred_element_type=F32)
              for h in heads]
        logs = [_log_sigmoid_pair(z * scale) for z in zz]
        lb = [p[0] for p in logs]
        l1 = [jnp.where(allowed, p[1], 0.0) if diagonal else p[1] for p in logs]
        sums = [_suffix_and_total(x, sfx) for x in l1]
        if diagonal:
            a = [jnp.where(allowed, jnp.exp(lb[h] + sums[h][0]), 0.0) for h in heads]
        else:
            a = [jnp.exp(lb[h] + sums[h][0] + run_ref[h]) for h in heads]
        parts = [jnp.dot(a[h].astype(BF16), v_ref[pl.ds(off, T), sl(h)], preferred_element_type=F32) for h in heads]
        for h in heads:
            if diagonal:
                acc_ref[h] = parts[h]
                run_ref[h] = sums[h][1]
            else:
                acc_ref[h] += parts[h]
                run_ref[h] += sums[h][1]

    tile_step(qi, True)

    def body(t, carry):
        tile_step(qi - 1 - t, False)
        return carry

    lax.fori_loop(0, qi, body, 0)
    for h in range(SB_HEADS):
        o_ref[:, h * HEAD_DIM:(h + 1) * HEAD_DIM] = acc_ref[h].astype(o_ref.dtype)


def sb_prompt_attention(qn, kb, vb, *, nb, seq):
    nq = seq // CHUNK
    return pl.pallas_call(
        _sb_prompt_kernel,
        grid=(nb, nq),
        in_specs=[
            pl.BlockSpec((CHUNK, W_SB), lambda b, i: (b * nq + i, 0)),
            pl.BlockSpec((seq, W_SB), lambda b, i: (b, 0)),
            pl.BlockSpec((seq, W_SB), lambda b, i: (b, 0)),
        ],
        out_specs=pl.BlockSpec((CHUNK, W_SB), lambda b, i: (b * nq + i, 0)),
        out_shape=jax.ShapeDtypeStruct((nb * seq, W_SB), BF16),
        scratch_shapes=[pltpu.VMEM((SB_HEADS, CHUNK, HEAD_DIM), F32), pltpu.VMEM((SB_HEADS, CHUNK, CHUNK), F32)],
        compiler_params=_params(("parallel", "arbitrary")),
        name="sb_prompt_attention",
    )(qn, kb, vb)


def _sb_decode_kernel(pt_ref, q_ref, *refs, n_steps):
    G = DECODE_PAGES_PER_STEP
    k_refs, v_refs = refs[:G], refs[G:2 * G]
    o_ref, acc_ref, run_ref = refs[2 * G:]
    s = pl.program_id(1)

    @pl.when(s == 0)
    def _():
        acc_ref[...] = jnp.zeros_like(acc_ref)
        run_ref[...] = jnp.zeros_like(run_ref)

    P = PAGE_SIZE
    q = q_ref[0]
    head_row = lax.broadcasted_iota(jnp.int32, (SB_HEADS, P), 0)
    sfx = _suffix_matrix(P)
    head_rows = lambda ref, h: ref[pl.ds(h, P, stride=SB_HEADS), :].astype(BF16)

    terms = []
    for r in range(G):
        zz = jnp.zeros((SB_HEADS, P), F32)
        for h in range(SB_HEADS):
            res = lax.dot_general(q, head_rows(k_refs[r], h), _NT, preferred_element_type=F32)
            zz = jnp.where(head_row == h, res, zz)
        lb, l1 = _log_sigmoid_pair(zz * (HEAD_DIM ** -0.5))
        suf, tot = _suffix_and_total(l1, sfx)
        terms.append((lb + suf, tot))

    run = run_ref[...]
    acc = acc_ref[...]
    for r in range(G):
        a = jnp.exp(terms[r][0] + run).astype(BF16)
        run = run + terms[r][1]
        for h in range(SB_HEADS):
            res = jnp.dot(a, head_rows(v_refs[r], h), preferred_element_type=F32)
            acc = acc + jnp.where(head_row == h, res, 0.0)
    run_ref[...] = run
    acc_ref[...] = acc

    @pl.when(s == n_steps - 1)
    def _():
        o_ref[0] = acc


def sb_decode_attention(qn, cache_k, cache_v, pages):
    nb, n_pages = pages.shape
    G = DECODE_PAGES_PER_STEP
    n_steps = n_pages // G
    page = lambda r: pl.BlockSpec((None, PAGE_SIZE * SB_HEADS, HEAD_DIM),
                                  lambda b, s, pt: (pt[b, n_pages - 1 - (s * G + r)], 0, 0))
    grid_spec = pltpu.PrefetchScalarGridSpec(
        num_scalar_prefetch=1,
        grid=(nb, n_steps),
        in_specs=[pl.BlockSpec((1, SB_HEADS, HEAD_DIM), lambda b, s, pt: (b, 0, 0))]
        + [page(r) for r in range(G)] * 2,
        out_specs=pl.BlockSpec((1, SB_HEADS, HEAD_DIM), lambda b, s, pt: (b, 0, 0)),
        scratch_shapes=[pltpu.VMEM((SB_HEADS, HEAD_DIM), F32), pltpu.VMEM((SB_HEADS, PAGE_SIZE), F32)],
    )
    out = pl.pallas_call(
        functools.partial(_sb_decode_kernel, n_steps=n_steps),
        grid_spec=grid_spec,
        out_shape=jax.ShapeDtypeStruct((nb, SB_HEADS, HEAD_DIM), F32),
        compiler_params=_params(("parallel", "arbitrary")),
        name="sb_decode_attention",
    )(pages, qn.reshape(nb, SB_HEADS, HEAD_DIM), *([cache_k] * G), *([cache_v] * G))
    return out.reshape(nb, W_SB).astype(BF16)


def _dense_in(l, xp, xs, norm_w, w_a, w_dt, w_b):
    hp, hs = rmsnorm_bf16(xp, norm_w), rmsnorm_bf16(xs, norm_w)
    a = matmul(hp, hs, w_a, l, tm=1024, tn=1024, tk=D_MODEL, name="in_proj_a")
    dt = matmul(hp, hs, w_dt, l, tm=1024, tn=LANES, tk=D_MODEL, name="dt_proj")
    b = matmul(hp, hs, w_b, l, tm=1024, tn=1024, tk=D_MODEL, name="in_proj_b")
    return (a[0], dt[0], b[0]), (a[1], dt[1], b[1])


def _dense_out(l, xp, xs, mix_p, mix_s, w_out, norm2_w, w_ff1, w_ff2):
    xp, xs = matmul(mix_p, mix_s, w_out, l, tm=1024, tn=512, tk=D_MODEL, epilogue="residual", residual=(xp, xs),
                    name="out_proj")
    hp, hs = rmsnorm_bf16(xp, norm2_w), rmsnorm_bf16(xs, norm2_w)
    fp, fs = matmul(hp, hs, w_ff1, l, tm=1024, tn=1024, tk=D_MODEL, out_dtype=BF16, epilogue="relu2", name="ffn_up")
    return matmul(fp, fs, w_ff2, l, tm=1024, tn=1024, tk=2048, epilogue="residual", residual=(xp, xs), name="ffn_down")


def kernel(x_prompt, x_sample, cache_k, cache_v, state_ssm, state_conv, state_pool, state_ret, page_table, norm1_w, w_in, conv_w, conv_b, dt_bias, a_log, d_skip, ssd_norm_w, pool_w, pool_scale, ret_gn_w, sb_q_norm, sb_k_norm, w_out, norm2_w, w_ff1, w_ff2):
    bp, seq, _ = x_prompt.shape
    bs = x_sample.shape[0]
    n_phys = cache_k.shape[1]
    past_len = page_table.shape[1] * PAGE_SIZE
    ncp = seq // CHUNK
    xp = x_prompt.reshape(bp * seq, D_MODEL)
    xs = x_sample.reshape(bs, D_MODEL)
    ck = cache_k.reshape(DEPTH * n_phys, PAGE_SIZE * SB_HEADS, HEAD_DIM)
    cv = cache_v.reshape(DEPTH * n_phys, PAGE_SIZE * SB_HEADS, HEAD_DIM)
    zeros = lambda *shape: jnp.zeros(shape, F32)
    outs = {name: [] for name in ("kp", "vp", "ks", "vs", "sp", "ss", "cp", "cs", "pp", "ps", "rp", "rs")}

    dt_off = PROJ_A
    w_a = w_in[:, :, :dt_off].astype(BF16)
    w_dt = jnp.pad(w_in[:, :, dt_off:dt_off + SSD_HEADS], ((0, 0), (0, 0), (0, LANES - SSD_HEADS))).astype(BF16)
    w_b = w_in[:, :, dt_off + SSD_HEADS:].astype(BF16)
    w_o, w_1, w_2 = w_out.astype(BF16), w_ff1.astype(BF16), w_ff2.astype(BF16)
    padded = lambda a: jnp.pad(a[:, None, :], ((0, 0), (0, CHUNK - 1), (0, 0))).reshape(bs * CHUNK, a.shape[1])
    first = lambda y: y.reshape(bs, CHUNK, -1)[:, 0, :]

    for l in range(DEPTH):
        ssd_w = (conv_w[l], conv_b[l], dt_bias[l], a_log[l], d_skip[l], ssd_norm_w[l])
        (pa, dt_raw, pb), (pa_s, dt_s, pb_s) = _dense_in(l, xp, xs, norm1_w[l], w_a, w_dt, w_b)

        y_ssd, cp, sp = ssd_mixer(pa, dt_raw, zeros(bp, 8, SSD_CONV_CH), zeros(bp, SSD_HEADS, 64, SSD_STATE), *ssd_w,
                                  nb=bp, nc=ncp, n_valid=CHUNK)
        y_pool, pp = pool_mixer(pb, zeros(bp, POOL_BUF + 1, W_POOL), pool_w[l], pool_scale[l],
                                nb=bp, nt=seq // 256, tl=256, n_valid=256, start=0)
        y_ret, rp = retention_mixer(pb, zeros(bp, RET_HEADS, HEAD_DIM, HEAD_DIM), ret_gn_w[l],
                                    nb=bp, nc=ncp, n_valid=CHUNK, start=0)
        qn, kn, kb, vf, vb = sb_prep(pb, sb_q_norm[l], sb_k_norm[l])
        y_sb = sb_prompt_attention(qn, kb, vb, nb=bp, seq=seq)
        mix_p = jnp.concatenate([y_ssd, y_pool, y_ret, y_sb], axis=1)
        outs["kp"].append(kn.reshape(bp, seq, SB_HEADS, HEAD_DIM))
        outs["vp"].append(vf.reshape(bp, seq, SB_HEADS, HEAD_DIM))
        outs["sp"].append(sp); outs["cp"].append(cp); outs["pp"].append(pp); outs["rp"].append(rp)

        pa_sp, dt_sp, pb_sp = padded(pa_s), padded(dt_s), padded(pb_s)
        y_ssd, cs, ss = ssd_mixer(pa_sp, dt_sp, jnp.pad(state_conv[l], ((0, 0), (5, 0), (0, 0))), state_ssm[l], *ssd_w,
                                  nb=bs, nc=1, n_valid=1)
        y_pool, ps = pool_mixer(pb_sp, jnp.pad(state_pool[l], ((0, 0), (1, 0), (0, 0))), pool_w[l], pool_scale[l],
                                nb=bs, nt=1, tl=CHUNK, n_valid=1, start=past_len)
        y_ret, rs = retention_mixer(pb_sp, state_ret[l], ret_gn_w[l], nb=bs, nc=1, n_valid=1, start=past_len)
        qn, kn, kb, vf, vb = sb_prep(pb_s, sb_q_norm[l], sb_k_norm[l])
        y_sb = sb_decode_attention(qn, ck, cv, page_table + l * n_phys)
        mix_s = jnp.concatenate([first(y_ssd), first(y_pool), first(y_ret), y_sb], axis=1)
        outs["ks"].append(kn.reshape(bs, 1, SB_HEADS, HEAD_DIM))
        outs["vs"].append(vf.reshape(bs, 1, SB_HEADS, HEAD_DIM))
        outs["ss"].append(ss); outs["cs"].append(cs); outs["ps"].append(ps); outs["rs"].append(rs)

        xp, xs = _dense_out(l, xp, xs, mix_p, mix_s, w_o, norm2_w[l], w_1, w_2)

    st = lambda name: jnp.stack(outs[name])
    return (xp.reshape(bp, seq, D_MODEL), xs.reshape(bs, 1, D_MODEL), st("kp"), st("vp"), st("ks"), st("vs"),
            st("sp"), st("ss"), st("cp"), st("cs"), st("pp"), st("ps"), st("rp"), st("rs"))
```

```python
import functools
import math

import jax
import jax.numpy as jnp
from jax import lax
from jax.experimental import pallas as pl
from jax.experimental.pallas import tpu as pltpu

F32 = jnp.float32
BF16 = jnp.bfloat16

D_MODEL = 4096
DEPTH = 4
PAGE_SIZE = 128
W_SSD = W_POOL = W_RET = W_SB = 1024
SSD_HEADS = 16
SSD_GROUPS = 4
SSD_STATE = 128
SSD_GN = SSD_GROUPS * SSD_STATE
SSD_CONV = 4
SSD_CONV_CH = W_SSD + 2 * SSD_GN
POOL_WINDOWS = (2, 4, 8, 16)
POOL_GC = W_POOL // len(POOL_WINDOWS)
POOL_BUF = max(POOL_WINDOWS) - 1
RET_HEADS = 8
HEAD_DIM = 128
SB_HEADS = 8
ROPE_BASE = 10000.0
D_FF = 4 * D_MODEL
NORM_EPS = 1e-6
CHUNK = 128
DECODE_PAGES_PER_STEP = 4
PROJ_A = W_SSD * 2 + SSD_GN * 2
PROJ_B_COLS = ("u", "rq", "rk", "rv", "rg", "sq", "sk", "sv")
PROJ_B = 1024 * len(PROJ_B_COLS)
BF16_SUBLANES = 16
LANES = 128
NEG_BIG = -1e30
VMEM_LIMIT = 56 * 1024 * 1024

_NT = (((1,), (1,)), ((), ()))


def _params(sem):
    return pltpu.CompilerParams(dimension_semantics=sem, vmem_limit_bytes=VMEM_LIMIT)


def _silu(x):
    return x / (1.0 + jnp.exp(-x))


def _softplus(x):
    return jnp.maximum(x, 0.0) + jnp.log1p(jnp.exp(-jnp.abs(x)))


def _dot(a, b):
    return jnp.dot(a.astype(BF16), b.astype(BF16), preferred_element_type=F32)


def _dot_nt(a, b):
    return lax.dot_general(a.astype(BF16), b.astype(BF16), _NT, preferred_element_type=F32)


def _rmsnorm_kernel(x_ref, w_ref, o_ref):
    x = x_ref[...]
    ms = jnp.mean(x * x, axis=-1, keepdims=True)
    o_ref[...] = (x * lax.rsqrt(ms + NORM_EPS) * w_ref[...]).astype(o_ref.dtype)


def rmsnorm_bf16(x, w):
    m, d = x.shape
    tr = min(m, 512)
    return pl.pallas_call(
        _rmsnorm_kernel,
        grid=(m // tr,),
        in_specs=[pl.BlockSpec((tr, d), lambda i: (i, 0)), pl.BlockSpec((1, d), lambda i: (0, 0))],
        out_specs=pl.BlockSpec((tr, d), lambda i: (i, 0)),
        out_shape=jax.ShapeDtypeStruct((m, d), BF16),
        compiler_params=_params(("parallel",)),
        name="rmsnorm",
    )(x, w.reshape(1, d))


def _mm_kernel(*refs, nk, epilogue, casts):
    xb_ref, xs_ref, w_ref = refs[:3]
    n_res = 2 if epilogue == "residual" else 0
    rb_ref, rs_ref = refs[3:5] if n_res else (None, None)
    n_in = 3 + n_res + len(casts)
    src_refs = refs[3 + n_res:n_in]
    ob_ref, os_ref = refs[n_in:n_in + 2]
    n_dst = sum(3 if c == "w_in" else 1 for c in casts)
    dst_refs = list(refs[n_in + 2:n_in + 2 + n_dst])
    accb_ref, accs_ref = refs[n_in + 2 + n_dst:] if nk > 1 else (None, None)
    k = pl.program_id(2)

    for kind, src_ref in zip(casts, src_refs):
        if kind == "w_in":
            dst_a, dst_dt, dst_b = dst_refs.pop(0), dst_refs.pop(0), dst_refs.pop(0)
            dst_a[...] = src_ref[:, :PROJ_A].astype(BF16)
            dst_dt[...] = src_ref[:, PROJ_A:PROJ_A + LANES].astype(BF16)
            dst_b[...] = src_ref[:, PROJ_A + SSD_HEADS:].astype(BF16)
        else:
            dst_refs.pop(0)[...] = src_ref[...].astype(BF16)

    def finish(acc, r_ref, o_ref):
        if epilogue == "relu2":
            r = jnp.maximum(acc, 0.0)
            acc = r * r
        elif epilogue == "residual":
            acc = acc + r_ref[...]
        o_ref[...] = acc.astype(o_ref.dtype)

    def run(x_ref, r_ref, o_ref, acc_ref):
        part = jnp.dot(x_ref[...], w_ref[...], preferred_element_type=F32)
        if nk == 1:
            finish(part, r_ref, o_ref)
            return

        @pl.when(k == 0)
        def _():
            acc_ref[...] = part

        @pl.when(k > 0)
        def _():
            acc_ref[...] += part

        @pl.when(k == nk - 1)
        def _():
            finish(acc_ref[...], r_ref, o_ref)

    run(xb_ref, rb_ref, ob_ref, accb_ref)

    @pl.when(pl.program_id(1) == 0)
    def _():
        run(xs_ref, rs_ref, os_ref, accs_ref)


def matmul(xb, xs, w, *, tm, tn, tk, out_dtype=F32, epilogue=None, residual=None, casts=(), name="matmul"):
    m, kdim = xb.shape
    ms = xs.shape[0]
    n = w.shape[1]
    tm, tn, tk = min(tm, m), min(tn, n), min(tk, kdim)
    nk = kdim // tk
    ni = m // tm
    n_steps = (n // tn) * ni * nk
    step = lambda j, i, k: (j * ni + i) * nk + k
    in_specs = [
        pl.BlockSpec((tm, tk), lambda j, i, k: (i, k)),
        pl.BlockSpec((ms, tk), lambda j, i, k: (0, k)),
        pl.BlockSpec((tk, tn), lambda j, i, k: (k, j)),
    ]
    args = [xb, xs, w]
    out_specs = [pl.BlockSpec((tm, tn), lambda j, i, k: (i, j)), pl.BlockSpec((ms, tn), lambda j, i, k: (0, j))]
    out_shape = [jax.ShapeDtypeStruct((m, n), out_dtype), jax.ShapeDtypeStruct((ms, n), out_dtype)]
    if epilogue == "residual":
        in_specs += [pl.BlockSpec((tm, tn), lambda j, i, k: (i, j)), pl.BlockSpec((ms, tn), lambda j, i, k: (0, j))]
        args += list(residual)
    for kind, src, layer in casts:
        rows, cols = src.shape[1:]
        rs = rows // n_steps
        assert rs * n_steps == rows and rs % BF16_SUBLANES == 0
        in_specs.append(pl.BlockSpec((None, rs, cols), lambda j, i, k, layer=layer: (layer, step(j, i, k), 0)))
        args.append(src)
        for width in ((PROJ_A, LANES, PROJ_B) if kind == "w_in" else (cols,)):
            out_specs.append(pl.BlockSpec((rs, width), lambda j, i, k: (step(j, i, k), 0)))
            out_shape.append(jax.ShapeDtypeStruct((rows, width), BF16))
    return pl.pallas_call(
        functools.partial(_mm_kernel, nk=nk, epilogue=epilogue, casts=tuple(c[0] for c in casts)),
        grid=(n // tn, ni, nk),
        in_specs=in_specs,
        out_specs=out_specs,
        out_shape=out_shape,
        scratch_shapes=[] if nk == 1 else [pltpu.VMEM((tm, tn), F32), pltpu.VMEM((ms, tn), F32)],
        compiler_params=_params(("arbitrary", "arbitrary", "arbitrary")),
        name=name,
    )(*args)


def _ssd_kernel(xs_ref, z_ref, bm_ref, cm_ref, dt_ref, cs_ref, s0_ref, cw_ref, cb_ref, dtb_ref, alog_ref,
                dsk_ref, nw_ref, y_ref, cnew_ref, snew_ref, ext_ref, s_ref, *, nc, n_valid):
    c = pl.program_id(1)
    C = CHUNK

    @pl.when(c == 0)
    def _():
        ext_ref[0:8, :] = cs_ref[0]
        s_ref[...] = s0_ref[0]

    ext_ref[8:8 + C, 0:W_SSD] = xs_ref[...]
    ext_ref[8:8 + C, W_SSD:W_SSD + SSD_GN] = bm_ref[...]
    ext_ref[8:8 + C, W_SSD + SSD_GN:] = cm_ref[...]
    cw = cw_ref[...]
    conv = cb_ref[...] + ext_ref[8:8 + C, :] * cw[3:4, :]
    for tap in range(SSD_CONV - 1):
        conv = conv + ext_ref[5 + tap:5 + tap + C, :] * cw[tap:tap + 1, :]

    @pl.when(c == nc - 1)
    def _():
        cnew_ref[0] = ext_ref[5 + n_valid:8 + n_valid, :]

    ext_ref[0:8, :] = ext_ref[C:C + 8, :]

    xbc = _silu(conv)
    x = xbc[:, :W_SSD]
    bmat = xbc[:, W_SSD:W_SSD + SSD_GN]
    cmat = xbc[:, W_SSD + SSD_GN:]

    dt = _softplus(dt_ref[...] + dtb_ref[...])
    if n_valid < C:
        dt = jnp.where(lax.broadcasted_iota(jnp.int32, (C, LANES), 0) < n_valid, dt, 0.0)
    da = dt * (-jnp.exp(alog_ref[...]))
    ri = lax.broadcasted_iota(jnp.int32, (C, C), 0)
    ci = lax.broadcasted_iota(jnp.int32, (C, C), 1)
    tril = ri >= ci
    cum = jnp.dot(tril.astype(F32), da, precision=lax.Precision.HIGHEST, preferred_element_type=F32)
    cum_t = cum.T
    end = cum[C - 1:C, :]
    e_end = jnp.exp(end)
    lo = lax.broadcasted_iota(jnp.int32, (C, LANES), 1) < 64
    row_lo = lax.broadcasted_iota(jnp.int32, (C, LANES), 0) < 64
    dsk = dsk_ref[...]

    def pick(v, h0):
        return jnp.where(lo, v[:, h0:h0 + 1], v[:, h0 + 1:h0 + 2])

    ys = []
    for g in range(SSD_GROUPS):
        bg = bmat[:, g * SSD_STATE:(g + 1) * SSD_STATE].astype(BF16)
        cg = cmat[:, g * SSD_STATE:(g + 1) * SSD_STATE].astype(BF16)
        cb = _dot_nt(cg, bg)
        for pr in range(2):
            p = 2 * g + pr
            h0 = 2 * p
            xp = x[:, p * LANES:(p + 1) * LANES]
            xdt = xp * pick(dt, h0)
            y_in = jnp.zeros((C, LANES), F32)
            for h, keep in ((h0, lo), (h0 + 1, jnp.logical_not(lo))):
                seg = cum[:, h:h + 1] - cum_t[h:h + 1, :]
                decay = jnp.exp(jnp.where(tril, seg, NEG_BIG))
                y_in = y_in + _dot(cb * decay, jnp.where(keep, xdt, 0.0))
            sp = s_ref[p]
            y_off = _dot_nt(cg, sp) * jnp.exp(pick(cum, h0))
            ys.append(y_in + y_off + jnp.where(lo[0:1], dsk[:, h0:h0 + 1], dsk[:, h0 + 1:h0 + 2]) * xp)
            w_end = jnp.exp(pick(jnp.broadcast_to(end, (C, LANES)) - cum, h0))
            s_scale = jnp.where(row_lo, e_end[:, h0:h0 + 1], e_end[:, h0 + 1:h0 + 2])
            s_ref[p] = sp * s_scale + _dot((xdt * w_end).T, bg)

    y = jnp.concatenate(ys, axis=1) * _silu(z_ref[...])
    ms = jnp.mean(y * y, axis=-1, keepdims=True)
    y_ref[...] = (y * lax.rsqrt(ms + NORM_EPS) * nw_ref[...]).astype(y_ref.dtype)

    @pl.when(c == nc - 1)
    def _():
        snew_ref[0] = s_ref[...]


def ssd_mixer(proj, dt_raw, conv_state8, ssm_state, conv_w, conv_b, dt_bias, a_log, d_skip, norm_w, *, nb, nc, n_valid):
    assert n_valid == CHUNK or nc == 1
    rows = nb * nc * CHUNK
    pad = lambda v: jnp.pad(v, (0, LANES - SSD_HEADS)).reshape(1, LANES)
    row = lambda b, c: b * nc + c
    full = lambda shape: pl.BlockSpec(shape, lambda b, c: (0,) * len(shape))
    n_pair = SSD_HEADS // 2
    y, conv_new, ssm_new = pl.pallas_call(
        functools.partial(_ssd_kernel, nc=nc, n_valid=n_valid),
        grid=(nb, nc),
        in_specs=[
            pl.BlockSpec((CHUNK, W_SSD), lambda b, c: (row(b, c), 0)),
            pl.BlockSpec((CHUNK, W_SSD), lambda b, c: (row(b, c), 1)),
            pl.BlockSpec((CHUNK, SSD_GN), lambda b, c: (row(b, c), 4)),
            pl.BlockSpec((CHUNK, SSD_GN), lambda b, c: (row(b, c), 5)),
            pl.BlockSpec((CHUNK, LANES), lambda b, c: (row(b, c), 0)),
            pl.BlockSpec((1, 8, SSD_CONV_CH), lambda b, c: (b, 0, 0)),
            pl.BlockSpec((1, n_pair, LANES, SSD_STATE), lambda b, c: (b, 0, 0, 0)),
            full((SSD_CONV, SSD_CONV_CH)),
            full((1, SSD_CONV_CH)),
            full((1, LANES)),
            full((1, LANES)),
            full((1, LANES)),
            full((1, W_SSD)),
        ],
        out_specs=[
            pl.BlockSpec((CHUNK, W_SSD), lambda b, c: (row(b, c), 0)),
            pl.BlockSpec((1, SSD_CONV - 1, SSD_CONV_CH), lambda b, c: (b, 0, 0)),
            pl.BlockSpec((1, n_pair, LANES, SSD_STATE), lambda b, c: (b, 0, 0, 0)),
        ],
        out_shape=[
            jax.ShapeDtypeStruct((rows, W_SSD), BF16),
            jax.ShapeDtypeStruct((nb, SSD_CONV - 1, SSD_CONV_CH), F32),
            jax.ShapeDtypeStruct((nb, n_pair, LANES, SSD_STATE), F32),
        ],
        scratch_shapes=[pltpu.VMEM((CHUNK + 8, SSD_CONV_CH), F32), pltpu.VMEM((n_pair, LANES, SSD_STATE), F32)],
        compiler_params=_params(("parallel", "arbitrary")),
        name="ssd_mixer",
    )(proj, proj, proj, proj, dt_raw, conv_state8, ssm_state.reshape(nb, n_pair, LANES, SSD_STATE),
      conv_w, conv_b.reshape(1, -1), pad(dt_bias), pad(a_log), pad(d_skip), norm_w.reshape(1, -1))
    return y, conv_new, ssm_new.reshape(nb, SSD_HEADS, 64, SSD_STATE)


def _pool_kernel(u_ref, buf_ref, pw_ref, ps_ref, y_ref, bnew_ref, ext_ref, *, nt, tl, n_valid, start):
    t = pl.program_id(1)
    H = POOL_BUF + 1

    @pl.when(t == 0)
    def _():
        ext_ref[0:H, :] = buf_ref[0]

    ext_ref[H:H + tl, :] = u_ref[...]
    pos = start + t * tl + lax.broadcasted_iota(jnp.int32, (tl, 1), 0)
    outs = []
    for g, w in enumerate(POOL_WINDOWS):
        cols = slice(g * POOL_GC, (g + 1) * POOL_GC)
        u = ext_ref[H:H + tl, cols]
        ssum = u
        for k in range(1, w):
            ssum = ssum + ext_ref[H - k:H - k + tl, cols]
        cnt = jnp.minimum(pos + 1, w).astype(F32)
        d = ssum / cnt - u
        outs.append(_dot(d, pw_ref[g]))
    y_ref[...] = (jnp.concatenate(outs, axis=1) * ps_ref[...]).astype(y_ref.dtype)

    @pl.when(t == nt - 1)
    def _():
        bnew_ref[0] = ext_ref[1 + n_valid:H + n_valid, :]

    ext_ref[0:H, :] = ext_ref[tl:tl + H, :]


def pool_mixer(proj, buf16, pool_w, pool_scale, *, nb, nt, tl, n_valid, start):
    assert n_valid == tl or nt == 1
    rows = nb * nt * tl
    ublk = PROJ_B_COLS.index("u")
    return pl.pallas_call(
        functools.partial(_pool_kernel, nt=nt, tl=tl, n_valid=n_valid, start=start),
        grid=(nb, nt),
        in_specs=[
            pl.BlockSpec((tl, W_POOL), lambda b, t: (b * nt + t, ublk)),
            pl.BlockSpec((1, POOL_BUF + 1, W_POOL), lambda b, t: (b, 0, 0)),
            pl.BlockSpec((len(POOL_WINDOWS), POOL_GC, POOL_GC), lambda b, t: (0, 0, 0)),
            pl.BlockSpec((1, W_POOL), lambda b, t: (0, 0)),
        ],
        out_specs=[
            pl.BlockSpec((tl, W_POOL), lambda b, t: (b * nt + t, 0)),
            pl.BlockSpec((1, POOL_BUF, W_POOL), lambda b, t: (b, 0, 0)),
        ],
        out_shape=[
            jax.ShapeDtypeStruct((rows, W_POOL), BF16),
            jax.ShapeDtypeStruct((nb, POOL_BUF, W_POOL), F32),
        ],
        scratch_shapes=[pltpu.VMEM((tl + POOL_BUF + 1, W_POOL), F32)],
        compiler_params=_params(("parallel", "arbitrary")),
        name="pool_mixer",
    )(proj, buf16, pool_w.astype(BF16), pool_scale.reshape(1, -1))


def _ret_kernel(q_ref, k_ref, v_ref, g_ref, cos_ref, sin_ref, s0_ref, gnw_ref, y_ref, snew_ref, s_ref, *, nc, n_valid):
    c = pl.program_id(1)
    C = CHUNK

    @pl.when(c == 0)
    def _():
        s_ref[...] = s0_ref[0]

    cos_f = cos_ref[...]
    sin_f = sin_ref[...]
    ri = lax.broadcasted_iota(jnp.int32, (C, C), 0)
    ci = lax.broadcasted_iota(jnp.int32, (C, C), 1)
    causal = ri >= ci
    diff = jnp.where(causal, ri - ci, 0).astype(F32)
    rowi = lax.broadcasted_iota(jnp.int32, (C, 1), 0)
    rowf = rowi.astype(F32)
    for h in range(RET_HEADS):
        lg = math.log1p(-(2.0 ** (-5 - h)))
        sl = slice(h * HEAD_DIM, (h + 1) * HEAD_DIM)
        qh, kh, vh = q_ref[:, sl], k_ref[:, sl], v_ref[:, sl]
        qr = (qh * cos_f + pltpu.roll(qh, HEAD_DIM // 2, 1) * sin_f) * (HEAD_DIM ** -0.5)
        kr = kh * cos_f + pltpu.roll(kh, HEAD_DIM // 2, 1) * sin_f
        intra = jnp.where(causal, jnp.exp(lg * diff), 0.0)
        s = _dot_nt(qr, kr) * intra
        q_dec = jnp.exp(lg * (rowf + 1.0))
        k_dec = jnp.where(rowi < n_valid, jnp.exp(lg * (n_valid - 1.0 - rowf)), 0.0)
        sh = s_ref[h]
        y = _dot(s, vh) + _dot(qr * q_dec, sh)
        s_ref[h] = sh * math.exp(lg * n_valid) + _dot((kr * k_dec).T, vh)
        mu = jnp.mean(y, axis=-1, keepdims=True)
        yc = y - mu
        var = jnp.mean(yc * yc, axis=-1, keepdims=True)
        yn = yc * lax.rsqrt(var + NORM_EPS) * gnw_ref[:, sl]
        y_ref[:, sl] = (_silu(g_ref[:, sl]) * yn).astype(y_ref.dtype)

    @pl.when(c == nc - 1)
    def _():
        snew_ref[0] = s_ref[...]


def _rope_tables(start, n):
    half = HEAD_DIM // 2
    inv = ROPE_BASE ** (-jnp.arange(half, dtype=F32) / half)
    ang = (start + jnp.arange(n)).astype(F32)[:, None] * inv[None, :]
    cos, sin = jnp.cos(ang), jnp.sin(ang)
    return jnp.concatenate([cos, cos], axis=1), jnp.concatenate([-sin, sin], axis=1)


def retention_mixer(proj, ret_state, gn_w, *, nb, nc, n_valid, start):
    assert n_valid == CHUNK or nc == 1
    rows = nb * nc * CHUNK
    cos_f, sin_f = _rope_tables(start, nc * CHUNK)
    col = lambda j: pl.BlockSpec((CHUNK, W_RET), lambda b, c: (b * nc + c, j))
    return pl.pallas_call(
        functools.partial(_ret_kernel, nc=nc, n_valid=n_valid),
        grid=(nb, nc),
        in_specs=[
            *[col(PROJ_B_COLS.index(name)) for name in ("rq", "rk", "rv", "rg")],
            pl.BlockSpec((CHUNK, HEAD_DIM), lambda b, c: (c, 0)),
            pl.BlockSpec((CHUNK, HEAD_DIM), lambda b, c: (c, 0)),
            pl.BlockSpec((1, RET_HEADS, HEAD_DIM, HEAD_DIM), lambda b, c: (b, 0, 0, 0)),
            pl.BlockSpec((1, W_RET), lambda b, c: (0, 0)),
        ],
        out_specs=[
            pl.BlockSpec((CHUNK, W_RET), lambda b, c: (b * nc + c, 0)),
            pl.BlockSpec((1, RET_HEADS, HEAD_DIM, HEAD_DIM), lambda b, c: (b, 0, 0, 0)),
        ],
        out_shape=[
            jax.ShapeDtypeStruct((rows, W_RET), BF16),
            jax.ShapeDtypeStruct((nb, RET_HEADS, HEAD_DIM, HEAD_DIM), F32),
        ],
        scratch_shapes=[pltpu.VMEM((RET_HEADS, HEAD_DIM, HEAD_DIM), F32)],
        compiler_params=_params(("parallel", "arbitrary")),
        name="retention_mixer",
    )(proj, proj, proj, proj, cos_f, sin_f, ret_state, gn_w.reshape(1, -1))


def _sb_prep_kernel(q_ref, k_ref, v_ref, qw_ref, kw_ref, qn_ref, kn_ref, kb_ref, vf_ref, vb_ref):
    for h in range(SB_HEADS):
        sl = slice(h * HEAD_DIM, (h + 1) * HEAD_DIM)
        q = q_ref[:, sl]
        k = k_ref[:, sl]
        qn = q * lax.rsqrt(jnp.mean(q * q, axis=-1, keepdims=True) + NORM_EPS) * qw_ref[...]
        kn = k * lax.rsqrt(jnp.mean(k * k, axis=-1, keepdims=True) + NORM_EPS) * kw_ref[...]
        qn_ref[:, sl] = qn.astype(BF16)
        kn_ref[:, sl] = kn
        kb_ref[:, sl] = kn.astype(BF16)
    v = v_ref[...]
    vf_ref[...] = v
    vb_ref[...] = v.astype(BF16)


def sb_prep(proj, q_norm, k_norm):
    m = proj.shape[0]
    tr = min(m, 256)
    col = lambda j: pl.BlockSpec((tr, W_SB), lambda i: (i, j))
    out = pl.BlockSpec((tr, W_SB), lambda i: (i, 0))
    sds = lambda dt: jax.ShapeDtypeStruct((m, W_SB), dt)
    return pl.pallas_call(
        _sb_prep_kernel,
        grid=(m // tr,),
        in_specs=[*[col(PROJ_B_COLS.index(name)) for name in ("sq", "sk", "sv")],
                  pl.BlockSpec((1, HEAD_DIM), lambda i: (0, 0)),
                  pl.BlockSpec((1, HEAD_DIM), lambda i: (0, 0))],
        out_specs=[out] * 5,
        out_shape=[sds(BF16), sds(F32), sds(BF16), sds(F32), sds(BF16)],
        compiler_params=_params(("parallel",)),
        name="sb_prep",
    )(proj, proj, proj, q_norm.reshape(1, -1), k_norm.reshape(1, -1))


def _log_sigmoid_pair(zz):
    lb = jnp.minimum(zz, 0.0) - jnp.log(1.0 + jnp.exp(-jnp.abs(zz)))
    return lb, lb - zz


def _suffix_matrix(t):
    row = lax.broadcasted_iota(jnp.int32, (t, 2 * t), 0)
    col = lax.broadcasted_iota(jnp.int32, (t, 2 * t), 1)
    return jnp.logical_or(col >= t, row > col).astype(BF16)


def _suffix_and_total(l1, suffix_matrix):
    t = l1.shape[1]
    hi = l1.astype(BF16)
    lo = (l1 - hi.astype(F32)).astype(BF16)
    both = (jnp.dot(hi, suffix_matrix, preferred_element_type=F32)
            + jnp.dot(lo, suffix_matrix, preferred_element_type=F32))
    return both[:, :t], both[:, t:]


def _sb_prompt_kernel(q_ref, k_ref, v_ref, o_ref, acc_ref, run_ref):
    qi = pl.program_id(1)
    T = CHUNK
    ri = lax.broadcasted_iota(jnp.int32, (T, T), 0)
    ci = lax.broadcasted_iota(jnp.int32, (T, T), 1)
    allowed = ci < ri
    sfx = _suffix_matrix(T)
    scale = HEAD_DIM ** -0.5

    def tile_step(kblk, diagonal):
        off = pl.multiple_of(kblk * T, T)
        heads = range(SB_HEADS)
        sl = lambda h: slice(h * HEAD_DIM, (h + 1) * HEAD_DIM)
        zz = [lax.dot_general(q_ref[:, sl(h)], k_ref[pl.ds(off, T), sl(h)], _NT, preferred_element_type=F32)
              for h in heads]
        logs = [_log_sigmoid_pair(z * scale) for z in zz]
        lb = [p[0] for p in logs]
        l1 = [jnp.where(allowed, p[1], 0.0) if diagonal else p[1] for p in logs]
        sums = [_suffix_and_total(x, sfx) for x in l1]
        if diagonal:
            a = [jnp.where(allowed, jnp.exp(lb[h] + sums[h][0]), 0.0) for h in heads]
        else:
            a = [jnp.exp(lb[h] + sums[h][0] + run_ref[h]) for h in heads]
        parts = [jnp.dot(a[h].astype(BF16), v_ref[pl.ds(off, T), sl(h)], preferred_element_type=F32) for h in heads]
        for h in heads:
            if diagonal:
                acc_ref[h] = parts[h]
                run_ref[h] = sums[h][1]
            else:
                acc_ref[h] += parts[h]
                run_ref[h] += sums[h][1]

    tile_step(qi, True)

    def body(t, carry):
        tile_step(qi - 1 - t, False)
        return carry

    lax.fori_loop(0, qi, body, 0)
    for h in range(SB_HEADS):
        o_ref[:, h * HEAD_DIM:(h + 1) * HEAD_DIM] = acc_ref[h].astype(o_ref.dtype)


def sb_prompt_attention(qn, kb, vb, *, nb, seq):
    nq = seq // CHUNK
    return pl.pallas_call(
        _sb_prompt_kernel,
        grid=(nb, nq),
        in_specs=[
            pl.BlockSpec((CHUNK, W_SB), lambda b, i: (b * nq + i, 0)),
            pl.BlockSpec((seq, W_SB), lambda b, i: (b, 0)),
            pl.BlockSpec((seq, W_SB), lambda b, i: (b, 0)),
        ],
        out_specs=pl.BlockSpec((CHUNK, W_SB), lambda b, i: (b * nq + i, 0)),
        out_shape=jax.ShapeDtypeStruct((nb * seq, W_SB), BF16),
        scratch_shapes=[pltpu.VMEM((SB_HEADS, CHUNK, HEAD_DIM), F32), pltpu.VMEM((SB_HEADS, CHUNK, CHUNK), F32)],
        compiler_params=_params(("parallel", "arbitrary")),
        name="sb_prompt_attention",
    )(qn, kb, vb)


def _sb_decode_kernel(pt_ref, q_ref, *refs, n_steps):
    G = DECODE_PAGES_PER_STEP
    k_refs, v_refs = refs[:G], refs[G:2 * G]
    o_ref, acc_ref, run_ref = refs[2 * G:]
    s = pl.program_id(1)

    @pl.when(s == 0)
    def _():
        acc_ref[...] = jnp.zeros_like(acc_ref)
        run_ref[...] = jnp.zeros_like(run_ref)

    P = PAGE_SIZE
    q = q_ref[0]
    head_row = lax.broadcasted_iota(jnp.int32, (SB_HEADS, P), 0)
    sfx = _suffix_matrix(P)
    head_rows = lambda ref, h: ref[pl.ds(h, P, stride=SB_HEADS), :].astype(BF16)

    scores = [[lax.dot_general(q, head_rows(k_refs[r], h), _NT, preferred_element_type=F32)
               for h in range(SB_HEADS)] for r in range(G)]
    logs = []
    for r in range(G):
        zz = scores[r][0]
        for h in range(1, SB_HEADS):
            zz = jnp.where(head_row == h, scores[r][h], zz)
        logs.append(_log_sigmoid_pair(zz * (HEAD_DIM ** -0.5)))
    sums = [_suffix_and_total(l1, sfx) for _, l1 in logs]
    run = run_ref[...]
    weights = []
    for r in range(G):
        weights.append(jnp.exp(logs[r][0] + sums[r][0] + run).astype(BF16))
        run = run + sums[r][1]
    run_ref[...] = run
    parts = [[jnp.dot(weights[r], head_rows(v_refs[r], h), preferred_element_type=F32)
              for h in range(SB_HEADS)] for r in range(G)]
    acc = acc_ref[...]
    for h in range(SB_HEADS):
        tot = parts[0][h]
        for r in range(1, G):
            tot = tot + parts[r][h]
        acc = acc + jnp.where(head_row == h, tot, 0.0)
    acc_ref[...] = acc

    @pl.when(s == n_steps - 1)
    def _():
        o_ref[0] = acc


def sb_decode_attention(qn, cache_k, cache_v, pages):
    nb, n_pages = pages.shape
    G = DECODE_PAGES_PER_STEP
    n_steps = n_pages // G
    page = lambda r: pl.BlockSpec((None, PAGE_SIZE * SB_HEADS, HEAD_DIM),
                                  lambda b, s, pt: (pt[b, n_pages - 1 - (s * G + r)], 0, 0))
    grid_spec = pltpu.PrefetchScalarGridSpec(
        num_scalar_prefetch=1,
        grid=(nb, n_steps),
        in_specs=[pl.BlockSpec((1, SB_HEADS, HEAD_DIM), lambda b, s, pt: (b, 0, 0))]
        + [page(r) for r in range(G)] * 2,
        out_specs=pl.BlockSpec((1, SB_HEADS, HEAD_DIM), lambda b, s, pt: (b, 0, 0)),
        scratch_shapes=[pltpu.VMEM((SB_HEADS, HEAD_DIM), F32), pltpu.VMEM((SB_HEADS, PAGE_SIZE), F32)],
    )
    out = pl.pallas_call(
        functools.partial(_sb_decode_kernel, n_steps=n_steps),
        grid_spec=grid_spec,
        out_shape=jax.ShapeDtypeStruct((nb, SB_HEADS, HEAD_DIM), F32),
        compiler_params=_params(("parallel", "arbitrary")),
        name="sb_decode_attention",
    )(pages, qn.reshape(nb, SB_HEADS, HEAD_DIM), *([cache_k] * G), *([cache_v] * G))
    return out.reshape(nb, W_SB).astype(BF16)


def _dense_in(l, xp, xs, norm_w, wa, wdt, wb, w_out):
    hp, hs = rmsnorm_bf16(xp, norm_w), rmsnorm_bf16(xs, norm_w)
    pa, pa_s = matmul(hp, hs, wa, tm=1024, tn=1024, tk=D_MODEL, name="in_proj_a")
    dt, dt_s = matmul(hp, hs, wdt, tm=1024, tn=LANES, tk=D_MODEL, name="dt_proj")
    pb, pb_s, wo = matmul(hp, hs, wb, tm=1024, tn=1024, tk=D_MODEL, casts=[("plain", w_out, l)], name="in_proj_b")
    return (pa, dt, pb), (pa_s, dt_s, pb_s), wo


def _dense_out(l, xp, xs, mix_p, mix_s, wo, norm2_w, w1, w_in, w_ff1, w_ff2):
    more = l + 1 < DEPTH
    xp, xs = matmul(mix_p, mix_s, wo, tm=1024, tn=512, tk=D_MODEL, epilogue="residual", residual=(xp, xs),
                    name="out_proj")
    hp, hs = rmsnorm_bf16(xp, norm2_w), rmsnorm_bf16(xs, norm2_w)
    fp, fs, w2, *w1_next = matmul(hp, hs, w1, tm=1024, tn=1024, tk=D_MODEL, out_dtype=BF16, epilogue="relu2",
                                  casts=[("plain", w_ff2, l)] + ([("plain", w_ff1, l + 1)] if more else []),
                                  name="ffn_up")
    xp, xs, *w_in_next = matmul(fp, fs, w2, tm=1024, tn=1024, tk=2048, epilogue="residual", residual=(xp, xs),
                                casts=[("w_in", w_in, l + 1)] if more else [], name="ffn_down")
    return xp, xs, w1_next, w_in_next


def kernel(x_prompt, x_sample, cache_k, cache_v, state_ssm, state_conv, state_pool, state_ret, page_table, norm1_w, w_in, conv_w, conv_b, dt_bias, a_log, d_skip, ssd_norm_w, pool_w, pool_scale, ret_gn_w, sb_q_norm, sb_k_norm, w_out, norm2_w, w_ff1, w_ff2):
    bp, seq, _ = x_prompt.shape
    bs = x_sample.shape[0]
    n_phys = cache_k.shape[1]
    past_len = page_table.shape[1] * PAGE_SIZE
    ncp = seq // CHUNK
    xp = x_prompt.reshape(bp * seq, D_MODEL)
    xs = x_sample.reshape(bs, D_MODEL)
    ck = cache_k.reshape(DEPTH * n_phys, PAGE_SIZE * SB_HEADS, HEAD_DIM)
    cv = cache_v.reshape(DEPTH * n_phys, PAGE_SIZE * SB_HEADS, HEAD_DIM)
    zeros = lambda *shape: jnp.zeros(shape, F32)
    outs = {name: [] for name in ("kp", "vp", "ks", "vs", "sp", "ss", "cp", "cs", "pp", "ps", "rp", "rs")}

    wa = w_in[0, :, :PROJ_A].astype(BF16)
    wdt = w_in[0, :, PROJ_A:PROJ_A + LANES].astype(BF16)
    wb = w_in[0, :, PROJ_A + SSD_HEADS:].astype(BF16)
    w1 = w_ff1[0].astype(BF16)
    padded = lambda a: jnp.pad(a[:, None, :], ((0, 0), (0, CHUNK - 1), (0, 0))).reshape(bs * CHUNK, a.shape[1])
    first = lambda y: y.reshape(bs, CHUNK, -1)[:, 0, :]

    for l in range(DEPTH):
        ssd_w = (conv_w[l], conv_b[l], dt_bias[l], a_log[l], d_skip[l], ssd_norm_w[l])
        (pa, dt_raw, pb), (pa_s, dt_s, pb_s), wo = _dense_in(l, xp, xs, norm1_w[l], wa, wdt, wb, w_out)

        y_ssd, cp, sp = ssd_mixer(pa, dt_raw, zeros(bp, 8, SSD_CONV_CH), zeros(bp, SSD_HEADS, 64, SSD_STATE), *ssd_w,
                                  nb=bp, nc=ncp, n_valid=CHUNK)
        y_pool, pp = pool_mixer(pb, zeros(bp, POOL_BUF + 1, W_POOL), pool_w[l], pool_scale[l],
                                nb=bp, nt=seq // 256, tl=256, n_valid=256, start=0)
        y_ret, rp = retention_mixer(pb, zeros(bp, RET_HEADS, HEAD_DIM, HEAD_DIM), ret_gn_w[l],
                                    nb=bp, nc=ncp, n_valid=CHUNK, start=0)
        qn, kn, kb, vf, vb = sb_prep(pb, sb_q_norm[l], sb_k_norm[l])
        y_sb = sb_prompt_attention(qn, kb, vb, nb=bp, seq=seq)
        mix_p = jnp.concatenate([y_ssd, y_pool, y_ret, y_sb], axis=1)
        outs["kp"].append(kn.reshape(bp, seq, SB_HEADS, HEAD_DIM))
        outs["vp"].append(vf.reshape(bp, seq, SB_HEADS, HEAD_DIM))
        outs["sp"].append(sp); outs["cp"].append(cp); outs["pp"].append(pp); outs["rp"].append(rp)

        pa_sp, dt_sp, pb_sp = padded(pa_s), padded(dt_s), padded(pb_s)
        y_ssd, cs, ss = ssd_mixer(pa_sp, dt_sp, jnp.pad(state_conv[l], ((0, 0), (5, 0), (0, 0))), state_ssm[l], *ssd_w,
                                  nb=bs, nc=1, n_valid=1)
        y_pool, ps = pool_mixer(pb_sp, jnp.pad(state_pool[l], ((0, 0), (1, 0), (0, 0))), pool_w[l], pool_scale[l],
                                nb=bs, nt=1, tl=CHUNK, n_valid=1, start=past_len)
        y_ret, rs = retention_mixer(pb_sp, state_ret[l], ret_gn_w[l], nb=bs, nc=1, n_valid=1, start=past_len)
        qn, kn, kb, vf, vb = sb_prep(pb_s, sb_q_norm[l], sb_k_norm[l])
        y_sb = sb_decode_attention(qn, ck, cv, page_table + l * n_phys)
        mix_s = jnp.concatenate([first(y_ssd), first(y_pool), first(y_ret), y_sb], axis=1)
        outs["ks"].append(kn.reshape(bs, 1, SB_HEADS, HEAD_DIM))
        outs["vs"].append(vf.reshape(bs, 1, SB_HEADS, HEAD_DIM))
        outs["ss"].append(ss); outs["cs"].append(cs); outs["ps"].append(ps); outs["rs"].append(rs)

        xp, xs, w1_next, w_in_next = _dense_out(l, xp, xs, mix_p, mix_s, wo, norm2_w[l], w1, w_in, w_ff1, w_ff2)
        if l + 1 < DEPTH:
            (w1,), (wa, wdt, wb) = w1_next, w_in_next

    st = lambda name: jnp.stack(outs[name])
    return (xp.reshape(bp, seq, D_MODEL), xs.reshape(bs, 1, D_MODEL), st("kp"), st("vp"), st("ks"), st("vs"),
            st("sp"), st("ss"), st("cp"), st("cs"), st("pp"), st("ps"), st("rp"), st("rs"))
```

```python
import functools
import math

import jax
import jax.numpy as jnp
from jax import lax
from jax.experimental import pallas as pl
from jax.experimental.pallas import tpu as pltpu

F32 = jnp.float32
BF16 = jnp.bfloat16

D_MODEL = 4096
DEPTH = 4
PAGE_SIZE = 128
W_SSD = W_POOL = W_RET = W_SB = 1024
SSD_HEADS = 16
SSD_GROUPS = 4
SSD_STATE = 128
SSD_GN = SSD_GROUPS * SSD_STATE
SSD_CONV = 4
SSD_CONV_CH = W_SSD + 2 * SSD_GN
POOL_WINDOWS = (2, 4, 8, 16)
POOL_GC = W_POOL // len(POOL_WINDOWS)
POOL_BUF = max(POOL_WINDOWS) - 1
RET_HEADS = 8
HEAD_DIM = 128
SB_HEADS = 8
ROPE_BASE = 10000.0
D_FF = 4 * D_MODEL
NORM_EPS = 1e-6
CHUNK = 128
DECODE_PAGES_PER_STEP = 4
PROJ_A = W_SSD * 2 + SSD_GN * 2
PROJ_B_COLS = ("u", "rq", "rk", "rv", "rg", "sq", "sk", "sv")
PROJ_B = 1024 * len(PROJ_B_COLS)
BF16_SUBLANES = 16
LANES = 128
NEG_BIG = -1e30
VMEM_LIMIT = 56 * 1024 * 1024

_NT = (((1,), (1,)), ((), ()))


def _params(sem):
    return pltpu.CompilerParams(dimension_semantics=sem, vmem_limit_bytes=VMEM_LIMIT)


def _silu(x):
    return x / (1.0 + jnp.exp(-x))


def _softplus(x):
    return jnp.maximum(x, 0.0) + jnp.log1p(jnp.exp(-jnp.abs(x)))


def _dot(a, b):
    return jnp.dot(a.astype(BF16), b.astype(BF16), preferred_element_type=F32)


def _dot_nt(a, b):
    return lax.dot_general(a.astype(BF16), b.astype(BF16), _NT, preferred_element_type=F32)


def _rmsnorm_kernel(x_ref, w_ref, o_ref):
    x = x_ref[...]
    ms = jnp.mean(x * x, axis=-1, keepdims=True)
    o_ref[...] = (x * lax.rsqrt(ms + NORM_EPS) * w_ref[...]).astype(o_ref.dtype)


def rmsnorm_bf16(x, w):
    m, d = x.shape
    tr = min(m, 512)
    return pl.pallas_call(
        _rmsnorm_kernel,
        grid=(m // tr,),
        in_specs=[pl.BlockSpec((tr, d), lambda i: (i, 0)), pl.BlockSpec((1, d), lambda i: (0, 0))],
        out_specs=pl.BlockSpec((tr, d), lambda i: (i, 0)),
        out_shape=jax.ShapeDtypeStruct((m, d), BF16),
        compiler_params=_params(("parallel",)),
        name="rmsnorm",
    )(x, w.reshape(1, d))


def _mm_kernel(*refs, nk, epilogue, casts):
    xb_ref, xs_ref, w_ref = refs[:3]
    n_res = 2 if epilogue == "residual" else 0
    rb_ref, rs_ref = refs[3:5] if n_res else (None, None)
    n_in = 3 + n_res + len(casts)
    src_refs = refs[3 + n_res:n_in]
    ob_ref, os_ref = refs[n_in:n_in + 2]
    n_dst = sum(3 if c == "w_in" else 1 for c in casts)
    dst_refs = refs[n_in + 2:n_in + 2 + n_dst]
    k = pl.program_id(2)
    _convert_blocks(casts, src_refs, dst_refs)

    def run(x_ref, r_ref, o_ref):
        part = jnp.dot(x_ref[...], w_ref[...], preferred_element_type=F32)
        if epilogue == "relu2":
            r = jnp.maximum(part, 0.0)
            o_ref[...] = (r * r).astype(o_ref.dtype)
        elif nk == 1:
            o_ref[...] = part if r_ref is None else part + r_ref[...]
        else:
            @pl.when(k == 0)
            def _():
                o_ref[...] = part if r_ref is None else part + r_ref[...]

            @pl.when(k > 0)
            def _():
                o_ref[...] += part

    run(xb_ref, rb_ref, ob_ref)

    @pl.when(pl.program_id(1) == 0)
    def _():
        run(xs_ref, rs_ref, os_ref)


def _convert_blocks(kinds, src_refs, dst_refs):
    dst_refs = list(dst_refs)
    for kind, src_ref in zip(kinds, src_refs):
        if kind == "w_in":
            dst_a, dst_dt, dst_b = dst_refs.pop(0), dst_refs.pop(0), dst_refs.pop(0)
            dst_a[...] = src_ref[:, :PROJ_A].astype(BF16)
            dst_dt[...] = src_ref[:, PROJ_A:PROJ_A + LANES].astype(BF16)
            dst_b[...] = src_ref[:, PROJ_A + SSD_HEADS:].astype(BF16)
        else:
            dst_refs.pop(0)[...] = src_ref[...].astype(BF16)


def _convert_specs(casts, n_steps, step):
    in_specs, args, out_specs, out_shape = [], [], [], []
    for kind, src, layer in casts:
        rows, cols = src.shape[1:]
        rs = rows // n_steps
        assert rs * n_steps == rows and rs % BF16_SUBLANES == 0
        in_specs.append(pl.BlockSpec((None, rs, cols), lambda *g, layer=layer: (layer, step(*g), 0)))
        args.append(src)
        for width in ((PROJ_A, LANES, PROJ_B) if kind == "w_in" else (cols,)):
            out_specs.append(pl.BlockSpec((rs, width), lambda *g: (step(*g), 0)))
            out_shape.append(jax.ShapeDtypeStruct((rows, width), BF16))
    return in_specs, args, out_specs, out_shape


def convert_weights(casts, n_steps):
    kinds = tuple(c[0] for c in casts)
    in_specs, args, out_specs, out_shape = _convert_specs(casts, n_steps, lambda i: i)

    def body(*refs):
        _convert_blocks(kinds, refs[:len(kinds)], refs[len(kinds):])

    return pl.pallas_call(body, grid=(n_steps,), in_specs=in_specs, out_specs=out_specs, out_shape=out_shape,
                          compiler_params=_params(("parallel",)), name="convert_weights")(*args)


def matmul(xb, xs, w, *, tm, tn, tk, out_dtype=F32, epilogue=None, residual=None, casts=(), name="matmul"):
    m, kdim = xb.shape
    ms = xs.shape[0]
    n = w.shape[1]
    tm, tn, tk = min(tm, m), min(tn, n), min(tk, kdim)
    nk = kdim // tk
    ni = m // tm
    n_steps = (n // tn) * ni * nk
    step = lambda j, i, k: (j * ni + i) * nk + k
    in_specs = [
        pl.BlockSpec((tm, tk), lambda j, i, k: (i, k)),
        pl.BlockSpec((ms, tk), lambda j, i, k: (0, k)),
        pl.BlockSpec((tk, tn), lambda j, i, k: (k, j)),
    ]
    args = [xb, xs, w]
    out_specs = [pl.BlockSpec((tm, tn), lambda j, i, k: (i, j)), pl.BlockSpec((ms, tn), lambda j, i, k: (0, j))]
    out_shape = [jax.ShapeDtypeStruct((m, n), out_dtype), jax.ShapeDtypeStruct((ms, n), out_dtype)]
    if epilogue == "residual":
        in_specs += [pl.BlockSpec((tm, tn), lambda j, i, k: (i, j)), pl.BlockSpec((ms, tn), lambda j, i, k: (0, j))]
        args += list(residual)
    assert nk == 1 or out_dtype == F32
    c_in, c_args, c_out, c_shape = _convert_specs(casts, n_steps, step)
    return pl.pallas_call(
        functools.partial(_mm_kernel, nk=nk, epilogue=epilogue, casts=tuple(c[0] for c in casts)),
        grid=(n // tn, ni, nk),
        in_specs=in_specs + c_in,
        out_specs=out_specs + c_out,
        out_shape=out_shape + c_shape,
        compiler_params=_params(("arbitrary", "arbitrary", "arbitrary")),
        name=name,
    )(*args, *c_args)


def _ssd_kernel(xs_ref, z_ref, bm_ref, cm_ref, dt_ref, cs_ref, s0_ref, cw_ref, cb_ref, dtb_ref, alog_ref,
                dsk_ref, nw_ref, y_ref, cnew_ref, snew_ref, ext_ref, s_ref, *, nc, n_valid):
    c = pl.program_id(1)
    C = CHUNK

    @pl.when(c == 0)
    def _():
        ext_ref[0:8, :] = cs_ref[0]
        s_ref[...] = s0_ref[0]

    ext_ref[8:8 + C, 0:W_SSD] = xs_ref[...]
    ext_ref[8:8 + C, W_SSD:W_SSD + SSD_GN] = bm_ref[...]
    ext_ref[8:8 + C, W_SSD + SSD_GN:] = cm_ref[...]
    cw = cw_ref[...]
    conv = cb_ref[...] + ext_ref[8:8 + C, :] * cw[3:4, :]
    for tap in range(SSD_CONV - 1):
        conv = conv + ext_ref[5 + tap:5 + tap + C, :] * cw[tap:tap + 1, :]

    @pl.when(c == nc - 1)
    def _():
        cnew_ref[0] = ext_ref[5 + n_valid:8 + n_valid, :]

    ext_ref[0:8, :] = ext_ref[C:C + 8, :]

    xbc = _silu(conv)
    x = xbc[:, :W_SSD]
    bmat = xbc[:, W_SSD:W_SSD + SSD_GN]
    cmat = xbc[:, W_SSD + SSD_GN:]

    dt = _softplus(dt_ref[...] + dtb_ref[...])
    if n_valid < C:
        dt = jnp.where(lax.broadcasted_iota(jnp.int32, (C, LANES), 0) < n_valid, dt, 0.0)
    da = dt * (-jnp.exp(alog_ref[...]))
    ri = lax.broadcasted_iota(jnp.int32, (C, C), 0)
    ci = lax.broadcasted_iota(jnp.int32, (C, C), 1)
    tril = ri >= ci
    cum = jnp.dot(tril.astype(F32), da, precision=lax.Precision.HIGHEST, preferred_element_type=F32)
    cum_t = cum.T
    end = cum[C - 1:C, :]
    e_end = jnp.exp(end)
    lo = lax.broadcasted_iota(jnp.int32, (C, LANES), 1) < 64
    row_lo = lax.broadcasted_iota(jnp.int32, (C, LANES), 0) < 64
    dsk = dsk_ref[...]

    def pick(v, h0):
        return jnp.where(lo, v[:, h0:h0 + 1], v[:, h0 + 1:h0 + 2])

    ys = []
    for g in range(SSD_GROUPS):
        bg = bmat[:, g * SSD_STATE:(g + 1) * SSD_STATE].astype(BF16)
        cg = cmat[:, g * SSD_STATE:(g + 1) * SSD_STATE].astype(BF16)
        cb = _dot_nt(cg, bg)
        for pr in range(2):
            p = 2 * g + pr
            h0 = 2 * p
            xp = x[:, p * LANES:(p + 1) * LANES]
            xdt = xp * pick(dt, h0)
            y_in = jnp.zeros((C, LANES), F32)
            for h, keep in ((h0, lo), (h0 + 1, jnp.logical_not(lo))):
                seg = cum[:, h:h + 1] - cum_t[h:h + 1, :]
                decay = jnp.exp(jnp.where(tril, seg, NEG_BIG))
                y_in = y_in + _dot(cb * decay, jnp.where(keep, xdt, 0.0))
            sp = s_ref[p]
            y_off = _dot_nt(cg, sp) * jnp.exp(pick(cum, h0))
            ys.append(y_in + y_off + jnp.where(lo[0:1], dsk[:, h0:h0 + 1], dsk[:, h0 + 1:h0 + 2]) * xp)
            w_end = jnp.exp(pick(jnp.broadcast_to(end, (C, LANES)) - cum, h0))
            s_scale = jnp.where(row_lo, e_end[:, h0:h0 + 1], e_end[:, h0 + 1:h0 + 2])
            s_ref[p] = sp * s_scale + _dot((xdt * w_end).T, bg)

    y = jnp.concatenate(ys, axis=1) * _silu(z_ref[...])
    ms = jnp.mean(y * y, axis=-1, keepdims=True)
    y_ref[...] = (y * lax.rsqrt(ms + NORM_EPS) * nw_ref[...]).astype(y_ref.dtype)

    @pl.when(c == nc - 1)
    def _():
        snew_ref[0] = s_ref[...]


def _mix_target(rows, mix, col, kernel_fn, n_in):
    if mix is None:
        return kernel_fn, jax.ShapeDtypeStruct((rows, 1024), BF16), 0, {}, [], []
    shape = jax.ShapeDtypeStruct((rows, D_MODEL), BF16)
    if isinstance(mix, str):
        return kernel_fn, shape, col, {}, [], []
    skipping = lambda *refs: kernel_fn(*refs[:n_in], *refs[n_in + 1:])
    return skipping, shape, col, {"input_output_aliases": {n_in: 0}}, [pl.BlockSpec(memory_space=pl.ANY)], [mix]


def ssd_mixer(proj, dt_raw, conv_state8, ssm_state, conv_w, conv_b, dt_bias, a_log, d_skip, norm_w, *, nb, nc, n_valid,
              mix=None):
    assert n_valid == CHUNK or nc == 1
    rows = nb * nc * CHUNK
    pad = lambda v: jnp.pad(v, (0, LANES - SSD_HEADS)).reshape(1, LANES)
    row = lambda b, c: b * nc + c
    full = lambda shape: pl.BlockSpec(shape, lambda b, c: (0,) * len(shape))
    n_pair = SSD_HEADS // 2
    kern, y_shape, ycol, alias, x_specs, x_args = _mix_target(
        rows, mix, 0, functools.partial(_ssd_kernel, nc=nc, n_valid=n_valid), 13)
    y, conv_new, ssm_new = pl.pallas_call(
        kern,
        grid=(nb, nc),
        **alias,
        in_specs=[
            pl.BlockSpec((CHUNK, W_SSD), lambda b, c: (row(b, c), 0)),
            pl.BlockSpec((CHUNK, W_SSD), lambda b, c: (row(b, c), 1)),
            pl.BlockSpec((CHUNK, SSD_GN), lambda b, c: (row(b, c), 4)),
            pl.BlockSpec((CHUNK, SSD_GN), lambda b, c: (row(b, c), 5)),
            pl.BlockSpec((CHUNK, LANES), lambda b, c: (row(b, c), 0)),
            pl.BlockSpec((1, 8, SSD_CONV_CH), lambda b, c: (b, 0, 0)),
            pl.BlockSpec((1, n_pair, LANES, SSD_STATE), lambda b, c: (b, 0, 0, 0)),
            full((SSD_CONV, SSD_CONV_CH)),
            full((1, SSD_CONV_CH)),
            full((1, LANES)),
            full((1, LANES)),
            full((1, LANES)),
            full((1, W_SSD)),
        ] + x_specs,
        out_specs=[
            pl.BlockSpec((CHUNK, W_SSD), lambda b, c: (row(b, c), ycol)),
            pl.BlockSpec((1, SSD_CONV - 1, SSD_CONV_CH), lambda b, c: (b, 0, 0)),
            pl.BlockSpec((1, n_pair, LANES, SSD_STATE), lambda b, c: (b, 0, 0, 0)),
        ],
        out_shape=[
            y_shape,
            jax.ShapeDtypeStruct((nb, SSD_CONV - 1, SSD_CONV_CH), F32),
            jax.ShapeDtypeStruct((nb, n_pair, LANES, SSD_STATE), F32),
        ],
        scratch_shapes=[pltpu.VMEM((CHUNK + 8, SSD_CONV_CH), F32), pltpu.VMEM((n_pair, LANES, SSD_STATE), F32)],
        compiler_params=_params(("parallel", "arbitrary")),
        name="ssd_mixer",
    )(proj, proj, proj, proj, dt_raw, conv_state8, ssm_state.reshape(nb, n_pair, LANES, SSD_STATE),
      conv_w, conv_b.reshape(1, -1), pad(dt_bias), pad(a_log), pad(d_skip), norm_w.reshape(1, -1), *x_args)
    return y, conv_new, ssm_new.reshape(nb, SSD_HEADS, 64, SSD_STATE)


def _pool_kernel(u_ref, buf_ref, pw_ref, ps_ref, y_ref, bnew_ref, ext_ref, *, nt, tl, n_valid, start):
    t = pl.program_id(1)
    H = POOL_BUF + 1

    @pl.when(t == 0)
    def _():
        ext_ref[0:H, :] = buf_ref[0]

    ext_ref[H:H + tl, :] = u_ref[...]
    pos = start + t * tl + lax.broadcasted_iota(jnp.int32, (tl, 1), 0)
    outs = []
    for g, w in enumerate(POOL_WINDOWS):
        cols = slice(g * POOL_GC, (g + 1) * POOL_GC)
        u = ext_ref[H:H + tl, cols]
        ssum = u
        for k in range(1, w):
            ssum = ssum + ext_ref[H - k:H - k + tl, cols]
        cnt = jnp.minimum(pos + 1, w).astype(F32)
        d = ssum / cnt - u
        outs.append(_dot(d, pw_ref[g]))
    y_ref[...] = (jnp.concatenate(outs, axis=1) * ps_ref[...]).astype(y_ref.dtype)

    @pl.when(t == nt - 1)
    def _():
        bnew_ref[0] = ext_ref[1 + n_valid:H + n_valid, :]

    ext_ref[0:H, :] = ext_ref[tl:tl + H, :]


def pool_mixer(proj, buf16, pool_w, pool_scale, *, nb, nt, tl, n_valid, start, mix=None):
    assert n_valid == tl or nt == 1
    rows = nb * nt * tl
    ublk = PROJ_B_COLS.index("u")
    kern, y_shape, ycol, alias, x_specs, x_args = _mix_target(
        rows, mix, 1, functools.partial(_pool_kernel, nt=nt, tl=tl, n_valid=n_valid, start=start), 4)
    return pl.pallas_call(
        kern,
        grid=(nb, nt),
        **alias,
        in_specs=[
            pl.BlockSpec((tl, W_POOL), lambda b, t: (b * nt + t, ublk)),
            pl.BlockSpec((1, POOL_BUF + 1, W_POOL), lambda b, t: (b, 0, 0)),
            pl.BlockSpec((len(POOL_WINDOWS), POOL_GC, POOL_GC), lambda b, t: (0, 0, 0)),
            pl.BlockSpec((1, W_POOL), lambda b, t: (0, 0)),
        ] + x_specs,
        out_specs=[
            pl.BlockSpec((tl, W_POOL), lambda b, t: (b * nt + t, ycol)),
            pl.BlockSpec((1, POOL_BUF, W_POOL), lambda b, t: (b, 0, 0)),
        ],
        out_shape=[y_shape, jax.ShapeDtypeStruct((nb, POOL_BUF, W_POOL), F32)],
        scratch_shapes=[pltpu.VMEM((tl + POOL_BUF + 1, W_POOL), F32)],
        compiler_params=_params(("parallel", "arbitrary")),
        name="pool_mixer",
    )(proj, buf16, pool_w.astype(BF16), pool_scale.reshape(1, -1), *x_args)


def _ret_kernel(q_ref, k_ref, v_ref, g_ref, cos_ref, sin_ref, s0_ref, gnw_ref, y_ref, snew_ref, s_ref, *, nc, n_valid):
    c = pl.program_id(1)
    C = CHUNK

    @pl.when(c == 0)
    def _():
        s_ref[...] = s0_ref[0]

    cos_f = cos_ref[...]
    sin_f = sin_ref[...]
    ri = lax.broadcasted_iota(jnp.int32, (C, C), 0)
    ci = lax.broadcasted_iota(jnp.int32, (C, C), 1)
    causal = ri >= ci
    diff = jnp.where(causal, ri - ci, 0).astype(F32)
    rowi = lax.broadcasted_iota(jnp.int32, (C, 1), 0)
    rowf = rowi.astype(F32)
    for h in range(RET_HEADS):
        lg = math.log1p(-(2.0 ** (-5 - h)))
        sl = slice(h * HEAD_DIM, (h + 1) * HEAD_DIM)
        qh, kh, vh = q_ref[:, sl], k_ref[:, sl], v_ref[:, sl]
        qr = (qh * cos_f + pltpu.roll(qh, HEAD_DIM // 2, 1) * sin_f) * (HEAD_DIM ** -0.5)
        kr = kh * cos_f + pltpu.roll(kh, HEAD_DIM // 2, 1) * sin_f
        intra = jnp.where(causal, jnp.exp(lg * diff), 0.0)
        s = _dot_nt(qr, kr) * intra
        q_dec = jnp.exp(lg * (rowf + 1.0))
        k_dec = jnp.where(rowi < n_valid, jnp.exp(lg * (n_valid - 1.0 - rowf)), 0.0)
        sh = s_ref[h]
        y = _dot(s, vh) + _dot(qr * q_dec, sh)
        s_ref[h] = sh * math.exp(lg * n_valid) + _dot((kr * k_dec).T, vh)
        mu = jnp.mean(y, axis=-1, keepdims=True)
        yc = y - mu
        var = jnp.mean(yc * yc, axis=-1, keepdims=True)
        yn = yc * lax.rsqrt(var + NORM_EPS) * gnw_ref[:, sl]
        y_ref[:, sl] = (_silu(g_ref[:, sl]) * yn).astype(y_ref.dtype)

    @pl.when(c == nc - 1)
    def _():
        snew_ref[0] = s_ref[...]


def _rope_tables(start, n):
    half = HEAD_DIM // 2
    inv = ROPE_BASE ** (-jnp.arange(half, dtype=F32) / half)
    ang = (start + jnp.arange(n)).astype(F32)[:, None] * inv[None, :]
    cos, sin = jnp.cos(ang), jnp.sin(ang)
    return jnp.concatenate([cos, cos], axis=1), jnp.concatenate([-sin, sin], axis=1)


def retention_mixer(proj, ret_state, gn_w, *, nb, nc, n_valid, start, mix=None):
    assert n_valid == CHUNK or nc == 1
    rows = nb * nc * CHUNK
    cos_f, sin_f = _rope_tables(start, nc * CHUNK)
    col = lambda j: pl.BlockSpec((CHUNK, W_RET), lambda b, c: (b * nc + c, j))
    kern, y_shape, ycol, alias, x_specs, x_args = _mix_target(
        rows, mix, 2, functools.partial(_ret_kernel, nc=nc, n_valid=n_valid), 8)
    return pl.pallas_call(
        kern,
        grid=(nb, nc),
        **alias,
        in_specs=[
            *[col(PROJ_B_COLS.index(name)) for name in ("rq", "rk", "rv", "rg")],
            pl.BlockSpec((CHUNK, HEAD_DIM), lambda b, c: (c, 0)),
            pl.BlockSpec((CHUNK, HEAD_DIM), lambda b, c: (c, 0)),
            pl.BlockSpec((1, RET_HEADS, HEAD_DIM, HEAD_DIM), lambda b, c: (b, 0, 0, 0)),
            pl.BlockSpec((1, W_RET), lambda b, c: (0, 0)),
        ] + x_specs,
        out_specs=[
            pl.BlockSpec((CHUNK, W_RET), lambda b, c: (b * nc + c, ycol)),
            pl.BlockSpec((1, RET_HEADS, HEAD_DIM, HEAD_DIM), lambda b, c: (b, 0, 0, 0)),
        ],
        out_shape=[y_shape, jax.ShapeDtypeStruct((nb, RET_HEADS, HEAD_DIM, HEAD_DIM), F32)],
        scratch_shapes=[pltpu.VMEM((RET_HEADS, HEAD_DIM, HEAD_DIM), F32)],
        compiler_params=_params(("parallel", "arbitrary")),
        name="retention_mixer",
    )(proj, proj, proj, proj, cos_f, sin_f, ret_state, gn_w.reshape(1, -1), *x_args)


def _sb_prep_kernel(q_ref, k_ref, v_ref, qw_ref, kw_ref, qn_ref, kn_ref, kb_ref, vf_ref, vb_ref):
    for h in range(SB_HEADS):
        sl = slice(h * HEAD_DIM, (h + 1) * HEAD_DIM)
        q = q_ref[:, sl]
        k = k_ref[:, sl]
        qn = q * lax.rsqrt(jnp.mean(q * q, axis=-1, keepdims=True) + NORM_EPS) * qw_ref[...]
        kn = k * lax.rsqrt(jnp.mean(k * k, axis=-1, keepdims=True) + NORM_EPS) * kw_ref[...]
        qn_ref[:, sl] = qn.astype(BF16)
        kn_ref[:, sl] = kn
        kb_ref[:, sl] = kn.astype(BF16)
    v = v_ref[...]
    vf_ref[...] = v
    vb_ref[...] = v.astype(BF16)


def sb_prep(proj, q_norm, k_norm, layer, kv_stacks=None):
    m = proj.shape[0]
    tr = min(m, 256)
    col = lambda j: pl.BlockSpec((tr, W_SB), lambda i: (i, j))
    out = pl.BlockSpec((tr, W_SB), lambda i: (i, 0))
    slot = pl.BlockSpec((None, tr, W_SB), lambda i: (layer, i, 0))
    sds = lambda dt: jax.ShapeDtypeStruct((m, W_SB), dt)
    stack = jax.ShapeDtypeStruct((DEPTH, m, W_SB), F32)
    n_in = 5
    kern, alias, x_specs, x_args = _sb_prep_kernel, {}, [], []
    if kv_stacks is not None:
        kern = lambda *refs: _sb_prep_kernel(*refs[:n_in], *refs[n_in + 2:])
        alias = {"input_output_aliases": {n_in: 1, n_in + 1: 3}}
        x_specs, x_args = [pl.BlockSpec(memory_space=pl.ANY)] * 2, list(kv_stacks)
    qn, k_stack, kb, v_stack, vb = pl.pallas_call(
        kern,
        grid=(m // tr,),
        **alias,
        in_specs=[*[col(PROJ_B_COLS.index(name)) for name in ("sq", "sk", "sv")],
                  pl.BlockSpec((1, HEAD_DIM), lambda i: (0, 0)),
                  pl.BlockSpec((1, HEAD_DIM), lambda i: (0, 0))] + x_specs,
        out_specs=[out, slot, out, slot, out],
        out_shape=[sds(BF16), stack, sds(BF16), stack, sds(BF16)],
        compiler_params=_params(("parallel",)),
        name="sb_prep",
    )(proj, proj, proj, q_norm.reshape(1, -1), k_norm.reshape(1, -1), *x_args)
    return qn, kb, vb, (k_stack, v_stack)


def _log_sigmoid_pair(zz):
    lb = jnp.minimum(zz, 0.0) - jnp.log(1.0 + jnp.exp(-jnp.abs(zz)))
    return lb, lb - zz


def _suffix_matrix(t):
    row = lax.broadcasted_iota(jnp.int32, (t, 2 * t), 0)
    col = lax.broadcasted_iota(jnp.int32, (t, 2 * t), 1)
    return jnp.logical_or(col >= t, row > col).astype(BF16)


def _suffix_and_total(l1, suffix_matrix):
    t = l1.shape[1]
    hi = l1.astype(BF16)
    lo = (l1 - hi.astype(F32)).astype(BF16)
    both = (jnp.dot(hi, suffix_matrix, preferred_element_type=F32)
            + jnp.dot(lo, suffix_matrix, preferred_element_type=F32))
    return both[:, :t], both[:, t:]


def _sb_prompt_kernel(q_ref, k_ref, v_ref, o_ref, acc_ref, run_ref):
    qi = pl.program_id(1)
    T = CHUNK
    ri = lax.broadcasted_iota(jnp.int32, (T, T), 0)
    ci = lax.broadcasted_iota(jnp.int32, (T, T), 1)
    allowed = ci < ri
    sfx = _suffix_matrix(T)
    scale = HEAD_DIM ** -0.5

    def tile_step(kblk, diagonal):
        off = pl.multiple_of(kblk * T, T)
        heads = range(SB_HEADS)
        sl = lambda h: slice(h * HEAD_DIM, (h + 1) * HEAD_DIM)
        zz = [lax.dot_general(q_ref[:, sl(h)], k_ref[pl.ds(off, T), sl(h)], _NT, preferred_element_type=F32)
              for h in heads]
        logs = [_log_sigmoid_pair(z * scale) for z in zz]
        lb = [p[0] for p in logs]
        l1 = [jnp.where(allowed, p[1], 0.0) if diagonal else p[1] for p in logs]
        sums = [_suffix_and_total(x, sfx) for x in l1]
        if diagonal:
            a = [jnp.where(allowed, jnp.exp(lb[h] + sums[h][0]), 0.0) for h in heads]
        else:
            a = [jnp.exp(lb[h] + sums[h][0] + run_ref[h]) for h in heads]
        parts = [jnp.dot(a[h].astype(BF16), v_ref[pl.ds(off, T), sl(h)], preferred_element_type=F32) for h in heads]
        for h in heads:
            if diagonal:
                acc_ref[h] = parts[h]
                run_ref[h] = sums[h][1]
            else:
                acc_ref[h] += parts[h]
                run_ref[h] += sums[h][1]

    tile_step(qi, True)

    def body(t, carry):
        tile_step(qi - 1 - t, False)
        return carry

    lax.fori_loop(0, qi, body, 0)
    for h in range(SB_HEADS):
        o_ref[:, h * HEAD_DIM:(h + 1) * HEAD_DIM] = acc_ref[h].astype(o_ref.dtype)


def sb_prompt_attention(qn, kb, vb, *, nb, seq, mix=None):
    nq = seq // CHUNK
    kern, y_shape, ycol, alias, x_specs, x_args = _mix_target(nb * seq, mix, 3, _sb_prompt_kernel, 3)
    return pl.pallas_call(
        kern,
        grid=(nb, nq),
        **alias,
        in_specs=[
            pl.BlockSpec((CHUNK, W_SB), lambda b, i: (b * nq + i, 0)),
            pl.BlockSpec((seq, W_SB), lambda b, i: (b, 0)),
            pl.BlockSpec((seq, W_SB), lambda b, i: (b, 0)),
        ] + x_specs,
        out_specs=pl.BlockSpec((CHUNK, W_SB), lambda b, i: (b * nq + i, ycol)),
        out_shape=y_shape,
        scratch_shapes=[pltpu.VMEM((SB_HEADS, CHUNK, HEAD_DIM), F32), pltpu.VMEM((SB_HEADS, CHUNK, CHUNK), F32)],
        compiler_params=_params(("parallel", "arbitrary")),
        name="sb_prompt_attention",
    )(qn, kb, vb, *x_args)


def _sb_decode_kernel(pt_ref, q_ref, *refs, n_steps):
    G = DECODE_PAGES_PER_STEP
    k_refs, v_refs = refs[:G], refs[G:2 * G]
    o_ref, acc_ref, run_ref = refs[2 * G:]
    s = pl.program_id(1)

    @pl.when(s == 0)
    def _():
        acc_ref[...] = jnp.zeros_like(acc_ref)
        run_ref[...] = jnp.zeros_like(run_ref)

    P = PAGE_SIZE
    q = q_ref[0]
    head_row = lax.broadcasted_iota(jnp.int32, (SB_HEADS, P), 0)
    sfx = _suffix_matrix(P)
    head_rows = lambda ref, h: ref[pl.ds(h, P, stride=SB_HEADS), :].astype(BF16)

    scores = [[lax.dot_general(q, head_rows(k_refs[r], h), _NT, preferred_element_type=F32)
               for h in range(SB_HEADS)] for r in range(G)]
    logs = []
    for r in range(G):
        zz = scores[r][0]
        for h in range(1, SB_HEADS):
            zz = jnp.where(head_row == h, scores[r][h], zz)
        logs.append(_log_sigmoid_pair(zz * (HEAD_DIM ** -0.5)))
    sums = [_suffix_and_total(l1, sfx) for _, l1 in logs]
    run = run_ref[...]
    weights = []
    for r in range(G):
        weights.append(jnp.exp(logs[r][0] + sums[r][0] + run).astype(BF16))
        run = run + sums[r][1]
    run_ref[...] = run
    parts = [[jnp.dot(weights[r], head_rows(v_refs[r], h), preferred_element_type=F32)
              for h in range(SB_HEADS)] for r in range(G)]
    acc = acc_ref[...]
    for h in range(SB_HEADS):
        tot = parts[0][h]
        for r in range(1, G):
            tot = tot + parts[r][h]
        acc = acc + jnp.where(head_row == h, tot, 0.0)
    acc_ref[...] = acc

    @pl.when(s == n_steps - 1)
    def _():
        o_ref[0] = acc


def sb_decode_attention(qn, cache_k, cache_v, pages):
    nb, n_pages = pages.shape
    G = DECODE_PAGES_PER_STEP
    n_steps = n_pages // G
    page = lambda r: pl.BlockSpec((None, PAGE_SIZE * SB_HEADS, HEAD_DIM),
                                  lambda b, s, pt: (pt[b, n_pages - 1 - (s * G + r)], 0, 0))
    grid_spec = pltpu.PrefetchScalarGridSpec(
        num_scalar_prefetch=1,
        grid=(nb, n_steps),
        in_specs=[pl.BlockSpec((1, SB_HEADS, HEAD_DIM), lambda b, s, pt: (b, 0, 0))]
        + [page(r) for r in range(G)] * 2,
        out_specs=pl.BlockSpec((1, SB_HEADS, HEAD_DIM), lambda b, s, pt: (b, 0, 0)),
        scratch_shapes=[pltpu.VMEM((SB_HEADS, HEAD_DIM), F32), pltpu.VMEM((SB_HEADS, PAGE_SIZE), F32)],
    )
    out = pl.pallas_call(
        functools.partial(_sb_decode_kernel, n_steps=n_steps),
        grid_spec=grid_spec,
        out_shape=jax.ShapeDtypeStruct((nb, SB_HEADS, HEAD_DIM), F32),
        compiler_params=_params(("parallel", "arbitrary")),
        name="sb_decode_attention",
    )(pages, qn.reshape(nb, SB_HEADS, HEAD_DIM), *([cache_k] * G), *([cache_v] * G))
    return out.reshape(nb, W_SB).astype(BF16)


def _dense_in(l, xp, xs, norm_w, wa, wdt, wb, w_out):
    hp, hs = rmsnorm_bf16(xp, norm_w), rmsnorm_bf16(xs, norm_w)
    pa, pa_s = matmul(hp, hs, wa, tm=1024, tn=1024, tk=D_MODEL, name="in_proj_a")
    dt, dt_s = matmul(hp, hs, wdt, tm=1024, tn=LANES, tk=D_MODEL, name="dt_proj")
    pb, pb_s, wo = matmul(hp, hs, wb, tm=1024, tn=1024, tk=D_MODEL, casts=[("plain", w_out, l)], name="in_proj_b")
    return (pa, dt, pb), (pa_s, dt_s, pb_s), wo


def _dense_out(l, xp, xs, mix_p, mix_s, wo, norm2_w, w1, w_in, w_ff1, w_ff2):
    more = l + 1 < DEPTH
    xp, xs = matmul(mix_p, mix_s, wo, tm=1024, tn=512, tk=D_MODEL, epilogue="residual", residual=(xp, xs),
                    name="out_proj")
    hp, hs = rmsnorm_bf16(xp, norm2_w), rmsnorm_bf16(xs, norm2_w)
    fp, fs, w2, *w1_next = matmul(hp, hs, w1, tm=1024, tn=1024, tk=D_MODEL, out_dtype=BF16, epilogue="relu2",
                                  casts=[("plain", w_ff2, l)] + ([("plain", w_ff1, l + 1)] if more else []),
                                  name="ffn_up")
    xp, xs, *w_in_next = matmul(fp, fs, w2, tm=1024, tn=1024, tk=2048, epilogue="residual", residual=(xp, xs),
                                casts=[("w_in", w_in, l + 1)] if more else [], name="ffn_down")
    return xp, xs, w1_next, w_in_next


def kernel(x_prompt, x_sample, cache_k, cache_v, state_ssm, state_conv, state_pool, state_ret, page_table, norm1_w, w_in, conv_w, conv_b, dt_bias, a_log, d_skip, ssd_norm_w, pool_w, pool_scale, ret_gn_w, sb_q_norm, sb_k_norm, w_out, norm2_w, w_ff1, w_ff2):
    bp, seq, _ = x_prompt.shape
    bs = x_sample.shape[0]
    n_phys = cache_k.shape[1]
    past_len = page_table.shape[1] * PAGE_SIZE
    ncp = seq // CHUNK
    xp = x_prompt.reshape(bp * seq, D_MODEL)
    xs = x_sample.reshape(bs, D_MODEL)
    ck = cache_k.reshape(DEPTH * n_phys, PAGE_SIZE * SB_HEADS, HEAD_DIM)
    cv = cache_v.reshape(DEPTH * n_phys, PAGE_SIZE * SB_HEADS, HEAD_DIM)
    zeros = lambda *shape: jnp.zeros(shape, F32)
    outs = {name: [] for name in ("sp", "ss", "cp", "cs", "pp", "ps", "rp", "rs")}
    kv_p = kv_s = None

    wa, wdt, wb, w1 = convert_weights([("w_in", w_in, 0), ("plain", w_ff1, 0)], n_steps=64)
    padded = lambda a: jnp.pad(a[:, None, :], ((0, 0), (0, CHUNK - 1), (0, 0))).reshape(bs * CHUNK, a.shape[1])
    first = lambda y: y.reshape(bs, CHUNK, -1)[:, 0, :]

    for l in range(DEPTH):
        ssd_w = (conv_w[l], conv_b[l], dt_bias[l], a_log[l], d_skip[l], ssd_norm_w[l])
        (pa, dt_raw, pb), (pa_s, dt_s, pb_s), wo = _dense_in(l, xp, xs, norm1_w[l], wa, wdt, wb, w_out)

        mix_p, cp, sp = ssd_mixer(pa, dt_raw, zeros(bp, 8, SSD_CONV_CH), zeros(bp, SSD_HEADS, 64, SSD_STATE), *ssd_w,
                                  nb=bp, nc=ncp, n_valid=CHUNK, mix="new")
        mix_p, pp = pool_mixer(pb, zeros(bp, POOL_BUF + 1, W_POOL), pool_w[l], pool_scale[l],
                               nb=bp, nt=seq // 256, tl=256, n_valid=256, start=0, mix=mix_p)
        mix_p, rp = retention_mixer(pb, zeros(bp, RET_HEADS, HEAD_DIM, HEAD_DIM), ret_gn_w[l],
                                    nb=bp, nc=ncp, n_valid=CHUNK, start=0, mix=mix_p)
        qn, kb, vb, kv_p = sb_prep(pb, sb_q_norm[l], sb_k_norm[l], l, kv_p)
        mix_p = sb_prompt_attention(qn, kb, vb, nb=bp, seq=seq, mix=mix_p)
        outs["sp"].append(sp); outs["cp"].append(cp); outs["pp"].append(pp); outs["rp"].append(rp)

        pa_sp, dt_sp, pb_sp = padded(pa_s), padded(dt_s), padded(pb_s)
        y_ssd, cs, ss = ssd_mixer(pa_sp, dt_sp, jnp.pad(state_conv[l], ((0, 0), (5, 0), (0, 0))), state_ssm[l], *ssd_w,
                                  nb=bs, nc=1, n_valid=1)
        y_pool, ps = pool_mixer(pb_sp, jnp.pad(state_pool[l], ((0, 0), (1, 0), (0, 0))), pool_w[l], pool_scale[l],
                                nb=bs, nt=1, tl=CHUNK, n_valid=1, start=past_len)
        y_ret, rs = retention_mixer(pb_sp, state_ret[l], ret_gn_w[l], nb=bs, nc=1, n_valid=1, start=past_len)
        qn, kb, vb, kv_s = sb_prep(pb_s, sb_q_norm[l], sb_k_norm[l], l, kv_s)
        y_sb = sb_decode_attention(qn, ck, cv, page_table + l * n_phys)
        mix_s = jnp.concatenate([first(y_ssd), first(y_pool), first(y_ret), y_sb], axis=1)
        outs["ss"].append(ss); outs["cs"].append(cs); outs["ps"].append(ps); outs["rs"].append(rs)

        xp, xs, w1_next, w_in_next = _dense_out(l, xp, xs, mix_p, mix_s, wo, norm2_w[l], w1, w_in, w_ff1, w_ff2)
        if l + 1 < DEPTH:
            (w1,), (wa, wdt, wb) = w1_next, w_in_next

    st = lambda name: jnp.stack(outs[name])
    kv = lambda a, nb, n: a.reshape(DEPTH, nb, n, SB_HEADS, HEAD_DIM)
    return (xp.reshape(bp, seq, D_MODEL), xs.reshape(bs, 1, D_MODEL),
            kv(kv_p[0], bp, seq), kv(kv_p[1], bp, seq), kv(kv_s[0], bs, 1), kv(kv_s[1], bs, 1),
            st("sp"), st("ss"), st("cp"), st("cs"), st("pp"), st("ps"), st("rp"), st("rs"))
```

```python
import functools
import math

import jax
import jax.numpy as jnp
from jax import lax
from jax.experimental import pallas as pl
from jax.experimental.pallas import tpu as pltpu

F32 = jnp.float32
BF16 = jnp.bfloat16

D_MODEL = 4096
DEPTH = 4
PAGE_SIZE = 128
W_SSD = W_POOL = W_RET = W_SB = 1024
SSD_HEADS = 16
SSD_GROUPS = 4
SSD_STATE = 128
SSD_GN = SSD_GROUPS * SSD_STATE
SSD_CONV = 4
SSD_CONV_CH = W_SSD + 2 * SSD_GN
POOL_WINDOWS = (2, 4, 8, 16)
POOL_GC = W_POOL // len(POOL_WINDOWS)
POOL_BUF = max(POOL_WINDOWS) - 1
RET_HEADS = 8
HEAD_DIM = 128
SB_HEADS = 8
ROPE_BASE = 10000.0
D_FF = 4 * D_MODEL
NORM_EPS = 1e-6
CHUNK = 128
DECODE_PAGES_PER_STEP = 8
PROJ_A = W_SSD * 2 + SSD_GN * 2
PROJ_B_COLS = ("u", "rq", "rk", "rv", "rg", "sq", "sk", "sv")
PROJ_B = 1024 * len(PROJ_B_COLS)
BF16_SUBLANES = 16
LANES = 128
NEG_BIG = -1e30
VMEM_LIMIT = 56 * 1024 * 1024

_NT = (((1,), (1,)), ((), ()))


def _params(sem):
    return pltpu.CompilerParams(dimension_semantics=sem, vmem_limit_bytes=VMEM_LIMIT)


def _silu(x):
    return x / (1.0 + jnp.exp(-x))


def _softplus(x):
    return jnp.maximum(x, 0.0) + jnp.log1p(jnp.exp(-jnp.abs(x)))


def _dot(a, b):
    return jnp.dot(a.astype(BF16), b.astype(BF16), preferred_element_type=F32)


def _dot_nt(a, b):
    return lax.dot_general(a.astype(BF16), b.astype(BF16), _NT, preferred_element_type=F32)


def _rmsnorm_kernel(x_ref, w_ref, o_ref):
    x = x_ref[...]
    ms = jnp.mean(x * x, axis=-1, keepdims=True)
    o_ref[...] = (x * lax.rsqrt(ms + NORM_EPS) * w_ref[...]).astype(o_ref.dtype)


def rmsnorm_bf16(x, w):
    m, d = x.shape
    tr = min(m, 512)
    return pl.pallas_call(
        _rmsnorm_kernel,
        grid=(m // tr,),
        in_specs=[pl.BlockSpec((tr, d), lambda i: (i, 0)), pl.BlockSpec((1, d), lambda i: (0, 0))],
        out_specs=pl.BlockSpec((tr, d), lambda i: (i, 0)),
        out_shape=jax.ShapeDtypeStruct((m, d), BF16),
        compiler_params=_params(("parallel",)),
        name="rmsnorm",
    )(x, w.reshape(1, d))


def _mm_kernel(*refs, nk, epilogue, casts, w_transposed):
    xb_ref, xs_ref, w_ref = refs[:3]
    n_res = 2 if epilogue == "residual" else 0
    rb_ref, rs_ref = refs[3:5] if n_res else (None, None)
    n_src = sum(_N_SRC[c] for c in casts)
    n_in = 3 + n_res + n_src
    src_refs = refs[3 + n_res:n_in]
    ob_ref, os_ref = refs[n_in:n_in + 2]
    dst_refs = refs[n_in + 2:]
    k = pl.program_id(2)
    _convert_blocks(casts, src_refs, dst_refs)

    def run(x_ref, r_ref, o_ref):
        if w_transposed:
            part = lax.dot_general(x_ref[...], w_ref[...], _NT, preferred_element_type=F32)
        else:
            part = jnp.dot(x_ref[...], w_ref[...], preferred_element_type=F32)
        if epilogue == "relu2":
            r = jnp.maximum(part, 0.0)
            o_ref[...] = (r * r).astype(o_ref.dtype)
        elif nk == 1:
            o_ref[...] = part if r_ref is None else part + r_ref[...]
        else:
            @pl.when(k == 0)
            def _():
                o_ref[...] = part if r_ref is None else part + r_ref[...]

            @pl.when(k > 0)
            def _():
                o_ref[...] += part

    run(xb_ref, rb_ref, ob_ref)

    @pl.when(pl.program_id(1) == 0)
    def _():
        run(xs_ref, rs_ref, os_ref)


W_IN_STEPS = 64
_A_ROWS = PROJ_A // W_IN_STEPS
_B_ROWS = PROJ_B // W_IN_STEPS
_DT_BLOCKS = LANES // BF16_SUBLANES
_N_SRC = {"plain": 1, "w_in": 4}
_N_DST = {"plain": 1, "w_in": 3}


def _convert_blocks(kinds, src_refs, dst_refs):
    src_refs, dst_refs = list(src_refs), list(dst_refs)
    for kind in kinds:
        if kind == "w_in":
            src_a, src_dt, src_b, src_tail = (src_refs.pop(0) for _ in range(4))
            dst_a, dst_dt, dst_b = (dst_refs.pop(0) for _ in range(3))
            dst_a[...] = src_a[...].astype(BF16)
            dst_dt[...] = src_dt[...].astype(BF16)
            dst_b[:_B_ROWS - SSD_HEADS, :] = src_b[SSD_HEADS:, :].astype(BF16)
            dst_b[_B_ROWS - SSD_HEADS:, :] = src_tail[...].astype(BF16)
        else:
            dst_refs.pop(0)[...] = src_refs.pop(0)[...].astype(BF16)


def _src_spec(n, cols, layer, block):
    return pl.BlockSpec((None, n, cols), lambda *g: (layer, block(*g), 0))


def _dst_spec(n, cols, block):
    return pl.BlockSpec((n, cols), lambda *g: (block(*g), 0))


def _convert_specs(casts, total, flat):
    in_specs, args, out_specs, out_shape = [], [], [], []
    for kind, src, layer in casts:
        rows, cols = src.shape[1:]
        shape = lambda n: jax.ShapeDtypeStruct((n, cols), BF16)
        if kind == "w_in":
            assert rows == PROJ_A + SSD_HEADS + PROJ_B and SSD_HEADS == BF16_SUBLANES and total % W_IN_STEPS == 0
            coarse = lambda *g: flat(*g) // (total // W_IN_STEPS)
            dt_block = lambda *g: jnp.minimum(coarse(*g), _DT_BLOCKS - 1)
            in_specs += [
                _src_spec(_A_ROWS, cols, layer, coarse),
                _src_spec(BF16_SUBLANES, cols, layer, lambda *g: PROJ_A // BF16_SUBLANES + dt_block(*g)),
                _src_spec(_B_ROWS, cols, layer, lambda *g: PROJ_A // _B_ROWS + coarse(*g)),
                _src_spec(SSD_HEADS, cols, layer,
                          lambda *g: (PROJ_A + _B_ROWS) // SSD_HEADS + coarse(*g) * (_B_ROWS // SSD_HEADS)),
            ]
            args += [src] * 4
            out_specs += [_dst_spec(_A_ROWS, cols, coarse), _dst_spec(BF16_SUBLANES, cols, dt_block),
                          _dst_spec(_B_ROWS, cols, coarse)]
            out_shape += [shape(PROJ_A), shape(LANES), shape(PROJ_B)]
        else:
            rs = rows // total
            assert rs * total == rows and rs % BF16_SUBLANES == 0
            in_specs.append(_src_spec(rs, cols, layer, flat))
            args.append(src)
            out_specs.append(_dst_spec(rs, cols, flat))
            out_shape.append(shape(rows))
    return in_specs, args, out_specs, out_shape


def convert_weights(casts):
    kinds = tuple(c[0] for c in casts)
    in_specs, args, out_specs, out_shape = _convert_specs(casts, W_IN_STEPS, lambda i: i)
    n_src = sum(_N_SRC[c] for c in kinds)

    def body(*refs):
        _convert_blocks(kinds, refs[:n_src], refs[n_src:])

    return pl.pallas_call(body, grid=(W_IN_STEPS,), in_specs=in_specs, out_specs=out_specs, out_shape=out_shape,
                          compiler_params=_params(("arbitrary",)), name="convert_weights")(*args)


def matmul(xb, xs, w, *, tm, tn, tk, out_dtype=F32, epilogue=None, residual=None, casts=(), w_transposed=False,
           name="matmul"):
    m, kdim = xb.shape
    ms = xs.shape[0]
    n = w.shape[0] if w_transposed else w.shape[1]
    tm, tn, tk = min(tm, m), min(tn, n), min(tk, kdim)
    nk = kdim // tk
    ni = m // tm
    n_steps = (n // tn) * ni * nk
    step = lambda j, i, k: (j * ni + i) * nk + k
    w_spec = pl.BlockSpec((tn, tk), lambda j, i, k: (j, k)) if w_transposed else pl.BlockSpec((tk, tn), lambda j, i, k: (k, j))
    in_specs = [
        pl.BlockSpec((tm, tk), lambda j, i, k: (i, k)),
        pl.BlockSpec((ms, tk), lambda j, i, k: (0, k)),
        w_spec,
    ]
    args = [xb, xs, w]
    out_specs = [pl.BlockSpec((tm, tn), lambda j, i, k: (i, j)), pl.BlockSpec((ms, tn), lambda j, i, k: (0, j))]
    out_shape = [jax.ShapeDtypeStruct((m, n), out_dtype), jax.ShapeDtypeStruct((ms, n), out_dtype)]
    if epilogue == "residual":
        in_specs += [pl.BlockSpec((tm, tn), lambda j, i, k: (i, j)), pl.BlockSpec((ms, tn), lambda j, i, k: (0, j))]
        args += list(residual)
    assert nk == 1 or out_dtype == F32
    c_in, c_args, c_out, c_shape = _convert_specs(casts, n_steps, step)
    return pl.pallas_call(
        functools.partial(_mm_kernel, nk=nk, epilogue=epilogue, casts=tuple(c[0] for c in casts),
                          w_transposed=w_transposed),
        grid=(n // tn, ni, nk),
        in_specs=in_specs + c_in,
        out_specs=out_specs + c_out,
        out_shape=out_shape + c_shape,
        compiler_params=_params(("arbitrary", "arbitrary", "arbitrary")),
        name=name,
    )(*args, *c_args)


def _ssd_kernel(xs_ref, z_ref, bm_ref, cm_ref, dt_ref, cs_ref, s0_ref, cw_ref, cb_ref, dtb_ref, alog_ref,
                dsk_ref, nw_ref, y_ref, cnew_ref, snew_ref, ext_ref, s_ref, *, nc, n_valid):
    c = pl.program_id(1)
    C = CHUNK

    @pl.when(c == 0)
    def _():
        ext_ref[0:8, :] = cs_ref[0]
        s_ref[...] = s0_ref[0]

    ext_ref[8:8 + C, 0:W_SSD] = xs_ref[...]
    ext_ref[8:8 + C, W_SSD:W_SSD + SSD_GN] = bm_ref[...]
    ext_ref[8:8 + C, W_SSD + SSD_GN:] = cm_ref[...]
    cw = cw_ref[...]
    conv = cb_ref[...] + ext_ref[8:8 + C, :] * cw[3:4, :]
    for tap in range(SSD_CONV - 1):
        conv = conv + ext_ref[5 + tap:5 + tap + C, :] * cw[tap:tap + 1, :]

    @pl.when(c == nc - 1)
    def _():
        cnew_ref[0] = ext_ref[5 + n_valid:8 + n_valid, :]

    ext_ref[0:8, :] = ext_ref[C:C + 8, :]

    xbc = _silu(conv)
    x = xbc[:, :W_SSD]
    bmat = xbc[:, W_SSD:W_SSD + SSD_GN]
    cmat = xbc[:, W_SSD + SSD_GN:]

    dt = _softplus(dt_ref[...] + dtb_ref[...])
    if n_valid < C:
        dt = jnp.where(lax.broadcasted_iota(jnp.int32, (C, LANES), 0) < n_valid, dt, 0.0)
    da = dt * (-jnp.exp(alog_ref[...]))
    ri = lax.broadcasted_iota(jnp.int32, (C, C), 0)
    ci = lax.broadcasted_iota(jnp.int32, (C, C), 1)
    tril = ri >= ci
    cum = jnp.dot(tril.astype(F32), da, precision=lax.Precision.HIGHEST, preferred_element_type=F32)
    cum_t = cum.T
    end = cum[C - 1:C, :]
    e_end = jnp.exp(end)
    lo = lax.broadcasted_iota(jnp.int32, (C, LANES), 1) < 64
    row_lo = lax.broadcasted_iota(jnp.int32, (C, LANES), 0) < 64
    dsk = dsk_ref[...]

    def pick(v, h0):
        return jnp.where(lo, v[:, h0:h0 + 1], v[:, h0 + 1:h0 + 2])

    ys = []
    for g in range(SSD_GROUPS):
        bg = bmat[:, g * SSD_STATE:(g + 1) * SSD_STATE].astype(BF16)
        cg = cmat[:, g * SSD_STATE:(g + 1) * SSD_STATE].astype(BF16)
        cb = _dot_nt(cg, bg)
        for pr in range(2):
            p = 2 * g + pr
            h0 = 2 * p
            xp = x[:, p * LANES:(p + 1) * LANES]
            xdt = xp * pick(dt, h0)
            y_in = jnp.zeros((C, LANES), F32)
            for h, keep in ((h0, lo), (h0 + 1, jnp.logical_not(lo))):
                seg = cum[:, h:h + 1] - cum_t[h:h + 1, :]
                decay = jnp.exp(jnp.where(tril, seg, NEG_BIG))
                y_in = y_in + _dot(cb * decay, jnp.where(keep, xdt, 0.0))
            sp = s_ref[p]
            y_off = _dot_nt(cg, sp) * jnp.exp(pick(cum, h0))
            ys.append(y_in + y_off + jnp.where(lo[0:1], dsk[:, h0:h0 + 1], dsk[:, h0 + 1:h0 + 2]) * xp)
            w_end = jnp.exp(pick(jnp.broadcast_to(end, (C, LANES)) - cum, h0))
            s_scale = jnp.where(row_lo, e_end[:, h0:h0 + 1], e_end[:, h0 + 1:h0 + 2])
            s_ref[p] = sp * s_scale + _dot((xdt * w_end).T, bg)

    y = jnp.concatenate(ys, axis=1) * _silu(z_ref[...])
    ms = jnp.mean(y * y, axis=-1, keepdims=True)
    y_ref[...] = (y * lax.rsqrt(ms + NORM_EPS) * nw_ref[...]).astype(y_ref.dtype)

    @pl.when(c == nc - 1)
    def _():
        snew_ref[0] = s_ref[...]


def _mix_target(rows, mix, col, kernel_fn, n_in):
    if mix is None:
        return kernel_fn, jax.ShapeDtypeStruct((rows, 1024), BF16), 0, {}, [], []
    shape = jax.ShapeDtypeStruct((rows, D_MODEL), BF16)
    if isinstance(mix, str):
        return kernel_fn, shape, col, {}, [], []
    skipping = lambda *refs: kernel_fn(*refs[:n_in], *refs[n_in + 1:])
    return skipping, shape, col, {"input_output_aliases": {n_in: 0}}, [pl.BlockSpec(memory_space=pl.ANY)], [mix]


def ssd_mixer(proj, dt_raw, conv_state8, ssm_state, conv_w, conv_b, dt_bias, a_log, d_skip, norm_w, *, nb, nc, n_valid,
              mix=None):
    assert n_valid == CHUNK or nc == 1
    rows = nb * nc * CHUNK
    pad = lambda v: jnp.pad(v, (0, LANES - SSD_HEADS)).reshape(1, LANES)
    row = lambda b, c: b * nc + c
    full = lambda shape: pl.BlockSpec(shape, lambda b, c: (0,) * len(shape))
    n_pair = SSD_HEADS // 2
    kern, y_shape, ycol, alias, x_specs, x_args = _mix_target(
        rows, mix, 0, functools.partial(_ssd_kernel, nc=nc, n_valid=n_valid), 13)
    y, conv_new, ssm_new = pl.pallas_call(
        kern,
        grid=(nb, nc),
        **alias,
        in_specs=[
            pl.BlockSpec((CHUNK, W_SSD), lambda b, c: (row(b, c), 0)),
            pl.BlockSpec((CHUNK, W_SSD), lambda b, c: (row(b, c), 1)),
            pl.BlockSpec((CHUNK, SSD_GN), lambda b, c: (row(b, c), 4)),
            pl.BlockSpec((CHUNK, SSD_GN), lambda b, c: (row(b, c), 5)),
            pl.BlockSpec((CHUNK, LANES), lambda b, c: (row(b, c), 0)),
            pl.BlockSpec((1, 8, SSD_CONV_CH), lambda b, c: (b, 0, 0)),
            pl.BlockSpec((1, n_pair, LANES, SSD_STATE), lambda b, c: (b, 0, 0, 0)),
            full((SSD_CONV, SSD_CONV_CH)),
            full((1, SSD_CONV_CH)),
            full((1, LANES)),
            full((1, LANES)),
            full((1, LANES)),
            full((1, W_SSD)),
        ] + x_specs,
        out_specs=[
            pl.BlockSpec((CHUNK, W_SSD), lambda b, c: (row(b, c), ycol)),
            pl.BlockSpec((1, SSD_CONV - 1, SSD_CONV_CH), lambda b, c: (b, 0, 0)),
            pl.BlockSpec((1, n_pair, LANES, SSD_STATE), lambda b, c: (b, 0, 0, 0)),
        ],
        out_shape=[
            y_shape,
            jax.ShapeDtypeStruct((nb, SSD_CONV - 1, SSD_CONV_CH), F32),
            jax.ShapeDtypeStruct((nb, n_pair, LANES, SSD_STATE), F32),
        ],
        scratch_shapes=[pltpu.VMEM((CHUNK + 8, SSD_CONV_CH), F32), pltpu.VMEM((n_pair, LANES, SSD_STATE), F32)],
        compiler_params=_params(("parallel", "arbitrary")),
        name="ssd_mixer",
    )(proj, proj, proj, proj, dt_raw, conv_state8, ssm_state.reshape(nb, n_pair, LANES, SSD_STATE),
      conv_w, conv_b.reshape(1, -1), pad(dt_bias), pad(a_log), pad(d_skip), norm_w.reshape(1, -1), *x_args)
    return y, conv_new, ssm_new.reshape(nb, SSD_HEADS, 64, SSD_STATE)


def _pool_kernel(u_ref, buf_ref, pw_ref, ps_ref, y_ref, bnew_ref, ext_ref, *, nt, tl, n_valid, start):
    t = pl.program_id(1)
    H = POOL_BUF + 1

    @pl.when(t == 0)
    def _():
        ext_ref[0:H, :] = buf_ref[0]

    ext_ref[H:H + tl, :] = u_ref[...]
    pos = start + t * tl + lax.broadcasted_iota(jnp.int32, (tl, 1), 0)
    outs = []
    for g, w in enumerate(POOL_WINDOWS):
        cols = slice(g * POOL_GC, (g + 1) * POOL_GC)
        u = ext_ref[H:H + tl, cols]
        ssum = u
        for k in range(1, w):
            ssum = ssum + ext_ref[H - k:H - k + tl, cols]
        cnt = jnp.minimum(pos + 1, w).astype(F32)
        d = ssum / cnt - u
        outs.append(_dot(d, pw_ref[g]))
    y_ref[...] = (jnp.concatenate(outs, axis=1) * ps_ref[...]).astype(y_ref.dtype)

    @pl.when(t == nt - 1)
    def _():
        bnew_ref[0] = ext_ref[1 + n_valid:H + n_valid, :]

    ext_ref[0:H, :] = ext_ref[tl:tl + H, :]


def pool_mixer(proj, buf16, pool_w, pool_scale, *, nb, nt, tl, n_valid, start, mix=None):
    assert n_valid == tl or nt == 1
    rows = nb * nt * tl
    ublk = PROJ_B_COLS.index("u")
    kern, y_shape, ycol, alias, x_specs, x_args = _mix_target(
        rows, mix, 1, functools.partial(_pool_kernel, nt=nt, tl=tl, n_valid=n_valid, start=start), 4)
    return pl.pallas_call(
        kern,
        grid=(nb, nt),
        **alias,
        in_specs=[
            pl.BlockSpec((tl, W_POOL), lambda b, t: (b * nt + t, ublk)),
            pl.BlockSpec((1, POOL_BUF + 1, W_POOL), lambda b, t: (b, 0, 0)),
            pl.BlockSpec((len(POOL_WINDOWS), POOL_GC, POOL_GC), lambda b, t: (0, 0, 0)),
            pl.BlockSpec((1, W_POOL), lambda b, t: (0, 0)),
        ] + x_specs,
        out_specs=[
            pl.BlockSpec((tl, W_POOL), lambda b, t: (b * nt + t, ycol)),
            pl.BlockSpec((1, POOL_BUF, W_POOL), lambda b, t: (b, 0, 0)),
        ],
        out_shape=[y_shape, jax.ShapeDtypeStruct((nb, POOL_BUF, W_POOL), F32)],
        scratch_shapes=[pltpu.VMEM((tl + POOL_BUF + 1, W_POOL), F32)],
        compiler_params=_params(("parallel", "arbitrary")),
        name="pool_mixer",
    )(proj, buf16, pool_w.astype(BF16), pool_scale.reshape(1, -1), *x_args)


def _ret_kernel(q_ref, k_ref, v_ref, g_ref, cos_ref, sin_ref, s0_ref, gnw_ref, y_ref, snew_ref, s_ref, *, nc, n_valid):
    c = pl.program_id(1)
    C = CHUNK

    @pl.when(c == 0)
    def _():
        s_ref[...] = s0_ref[0]

    cos_f = cos_ref[...]
    sin_f = sin_ref[...]
    ri = lax.broadcasted_iota(jnp.int32, (C, C), 0)
    ci = lax.broadcasted_iota(jnp.int32, (C, C), 1)
    causal = ri >= ci
    diff = jnp.where(causal, ri - ci, 0).astype(F32)
    rowi = lax.broadcasted_iota(jnp.int32, (C, 1), 0)
    rowf = rowi.astype(F32)
    for h in range(RET_HEADS):
        lg = math.log1p(-(2.0 ** (-5 - h)))
        sl = slice(h * HEAD_DIM, (h + 1) * HEAD_DIM)
        qh, kh, vh = q_ref[:, sl], k_ref[:, sl], v_ref[:, sl]
        qr = (qh * cos_f + pltpu.roll(qh, HEAD_DIM // 2, 1) * sin_f) * (HEAD_DIM ** -0.5)
        kr = kh * cos_f + pltpu.roll(kh, HEAD_DIM // 2, 1) * sin_f
        intra = jnp.where(causal, jnp.exp(lg * diff), 0.0)
        s = _dot_nt(qr, kr) * intra
        q_dec = jnp.exp(lg * (rowf + 1.0))
        k_dec = jnp.where(rowi < n_valid, jnp.exp(lg * (n_valid - 1.0 - rowf)), 0.0)
        sh = s_ref[h]
        y = _dot(s, vh) + _dot(qr * q_dec, sh)
        s_ref[h] = sh * math.exp(lg * n_valid) + _dot((kr * k_dec).T, vh)
        mu = jnp.mean(y, axis=-1, keepdims=True)
        yc = y - mu
        var = jnp.mean(yc * yc, axis=-1, keepdims=True)
        yn = yc * lax.rsqrt(var + NORM_EPS) * gnw_ref[:, sl]
        y_ref[:, sl] = (_silu(g_ref[:, sl]) * yn).astype(y_ref.dtype)

    @pl.when(c == nc - 1)
    def _():
        snew_ref[0] = s_ref[...]


def _rope_tables(start, n):
    half = HEAD_DIM // 2
    inv = ROPE_BASE ** (-jnp.arange(half, dtype=F32) / half)
    ang = (start + jnp.arange(n)).astype(F32)[:, None] * inv[None, :]
    cos, sin = jnp.cos(ang), jnp.sin(ang)
    return jnp.concatenate([cos, cos], axis=1), jnp.concatenate([-sin, sin], axis=1)


def retention_mixer(proj, ret_state, gn_w, *, nb, nc, n_valid, start, mix=None):
    assert n_valid == CHUNK or nc == 1
    rows = nb * nc * CHUNK
    cos_f, sin_f = _rope_tables(start, nc * CHUNK)
    col = lambda j: pl.BlockSpec((CHUNK, W_RET), lambda b, c: (b * nc + c, j))
    kern, y_shape, ycol, alias, x_specs, x_args = _mix_target(
        rows, mix, 2, functools.partial(_ret_kernel, nc=nc, n_valid=n_valid), 8)
    return pl.pallas_call(
        kern,
        grid=(nb, nc),
        **alias,
        in_specs=[
            *[col(PROJ_B_COLS.index(name)) for name in ("rq", "rk", "rv", "rg")],
            pl.BlockSpec((CHUNK, HEAD_DIM), lambda b, c: (c, 0)),
            pl.BlockSpec((CHUNK, HEAD_DIM), lambda b, c: (c, 0)),
            pl.BlockSpec((1, RET_HEADS, HEAD_DIM, HEAD_DIM), lambda b, c: (b, 0, 0, 0)),
            pl.BlockSpec((1, W_RET), lambda b, c: (0, 0)),
        ] + x_specs,
        out_specs=[
            pl.BlockSpec((CHUNK, W_RET), lambda b, c: (b * nc + c, ycol)),
            pl.BlockSpec((1, RET_HEADS, HEAD_DIM, HEAD_DIM), lambda b, c: (b, 0, 0, 0)),
        ],
        out_shape=[y_shape, jax.ShapeDtypeStruct((nb, RET_HEADS, HEAD_DIM, HEAD_DIM), F32)],
        scratch_shapes=[pltpu.VMEM((RET_HEADS, HEAD_DIM, HEAD_DIM), F32)],
        compiler_params=_params(("parallel", "arbitrary")),
        name="retention_mixer",
    )(proj, proj, proj, proj, cos_f, sin_f, ret_state, gn_w.reshape(1, -1), *x_args)


def _sb_prep_kernel(q_ref, k_ref, v_ref, qw_ref, kw_ref, qn_ref, kn_ref, kb_ref, vf_ref, vb_ref):
    for h in range(SB_HEADS):
        sl = slice(h * HEAD_DIM, (h + 1) * HEAD_DIM)
        q = q_ref[:, sl]
        k = k_ref[:, sl]
        qn = q * lax.rsqrt(jnp.mean(q * q, axis=-1, keepdims=True) + NORM_EPS) * qw_ref[...]
        kn = k * lax.rsqrt(jnp.mean(k * k, axis=-1, keepdims=True) + NORM_EPS) * kw_ref[...]
        qn_ref[:, sl] = qn.astype(BF16)
        kn_ref[:, sl] = kn
        kb_ref[:, sl] = kn.astype(BF16)
    v = v_ref[...]
    vf_ref[...] = v
    vb_ref[...] = v.astype(BF16)


def sb_prep(proj, q_norm, k_norm, layer, kv_stacks=None):
    m = proj.shape[0]
    tr = min(m, 256)
    col = lambda j: pl.BlockSpec((tr, W_SB), lambda i: (i, j))
    out = pl.BlockSpec((tr, W_SB), lambda i: (i, 0))
    slot = pl.BlockSpec((None, tr, W_SB), lambda i: (layer, i, 0))
    sds = lambda dt: jax.ShapeDtypeStruct((m, W_SB), dt)
    stack = jax.ShapeDtypeStruct((DEPTH, m, W_SB), F32)
    n_in = 5
    kern, alias, x_specs, x_args = _sb_prep_kernel, {}, [], []
    if kv_stacks is not None:
        kern = lambda *refs: _sb_prep_kernel(*refs[:n_in], *refs[n_in + 2:])
        alias = {"input_output_aliases": {n_in: 1, n_in + 1: 3}}
        x_specs, x_args = [pl.BlockSpec(memory_space=pl.ANY)] * 2, list(kv_stacks)
    qn, k_stack, kb, v_stack, vb = pl.pallas_call(
        kern,
        grid=(m // tr,),
        **alias,
        in_specs=[*[col(PROJ_B_COLS.index(name)) for name in ("sq", "sk", "sv")],
                  pl.BlockSpec((1, HEAD_DIM), lambda i: (0, 0)),
                  pl.BlockSpec((1, HEAD_DIM), lambda i: (0, 0))] + x_specs,
        out_specs=[out, slot, out, slot, out],
        out_shape=[sds(BF16), stack, sds(BF16), stack, sds(BF16)],
        compiler_params=_params(("parallel",)),
        name="sb_prep",
    )(proj, proj, proj, q_norm.reshape(1, -1), k_norm.reshape(1, -1), *x_args)
    return qn, kb, vb, (k_stack, v_stack)


def _log_sigmoid_pair(zz):
    lb = jnp.minimum(zz, 0.0) - jnp.log(1.0 + jnp.exp(-jnp.abs(zz)))
    return lb, lb - zz


def _suffix_matrix(t):
    row = lax.broadcasted_iota(jnp.int32, (t, 2 * t), 0)
    col = lax.broadcasted_iota(jnp.int32, (t, 2 * t), 1)
    return jnp.logical_or(col >= t, row > col).astype(BF16)


def _suffix_and_total(l1, suffix_matrix):
    t = l1.shape[1]
    hi = l1.astype(BF16)
    lo = (l1 - hi.astype(F32)).astype(BF16)
    both = (jnp.dot(hi, suffix_matrix, preferred_element_type=F32)
            + jnp.dot(lo, suffix_matrix, preferred_element_type=F32))
    return both[:, :t], both[:, t:]


def _sb_prompt_kernel(q_ref, k_ref, v_ref, o_ref, acc_ref, run_ref):
    qi = pl.program_id(1)
    T = CHUNK
    ri = lax.broadcasted_iota(jnp.int32, (T, T), 0)
    ci = lax.broadcasted_iota(jnp.int32, (T, T), 1)
    allowed = ci < ri
    sfx = _suffix_matrix(T)
    scale = HEAD_DIM ** -0.5

    def tiles_step(kblks, diagonal):
        offs = [pl.multiple_of(kb * T, T) for kb in kblks]
        tiles, heads = range(len(kblks)), range(SB_HEADS)
        sl = lambda h: slice(h * HEAD_DIM, (h + 1) * HEAD_DIM)
        zz = [[lax.dot_general(q_ref[:, sl(h)], k_ref[pl.ds(offs[t], T), sl(h)], _NT, preferred_element_type=F32)
               for h in heads] for t in tiles]
        logs = [[_log_sigmoid_pair(zz[t][h] * scale) for h in heads] for t in tiles]
        l1 = [[jnp.where(allowed, logs[t][h][1], 0.0) if diagonal else logs[t][h][1] for h in heads] for t in tiles]
        sums = [[_suffix_and_total(l1[t][h], sfx) for h in heads] for t in tiles]
        weights = [[None] * SB_HEADS for _ in tiles]
        for h in heads:
            run = None if diagonal else run_ref[h]
            for t in tiles:
                log_a = logs[t][h][0] + sums[t][h][0]
                a = jnp.exp(log_a if run is None else log_a + run)
                weights[t][h] = (jnp.where(allowed, a, 0.0) if diagonal else a).astype(BF16)
                run = sums[t][h][1] if run is None else run + sums[t][h][1]
            run_ref[h] = run
        parts = [[jnp.dot(weights[t][h], v_ref[pl.ds(offs[t], T), sl(h)], preferred_element_type=F32)
                  for h in heads] for t in tiles]
        for h in heads:
            tot = parts[0][h]
            for t in tiles[1:]:
                tot = tot + parts[t][h]
            acc_ref[h] = tot if diagonal else acc_ref[h] + tot

    tiles_step([qi], True)

    def body(t, carry):
        tiles_step([qi - 1 - 2 * t, qi - 2 - 2 * t], False)
        return carry

    lax.fori_loop(0, qi // 2, body, 0)

    @pl.when(qi % 2 == 1)
    def _():
        tiles_step([0], False)
    for h in range(SB_HEADS):
        o_ref[:, h * HEAD_DIM:(h + 1) * HEAD_DIM] = acc_ref[h].astype(o_ref.dtype)


def sb_prompt_attention(qn, kb, vb, *, nb, seq, mix=None):
    nq = seq // CHUNK
    kern, y_shape, ycol, alias, x_specs, x_args = _mix_target(nb * seq, mix, 3, _sb_prompt_kernel, 3)
    return pl.pallas_call(
        kern,
        grid=(nb, nq),
        **alias,
        in_specs=[
            pl.BlockSpec((CHUNK, W_SB), lambda b, i: (b * nq + i, 0)),
            pl.BlockSpec((seq, W_SB), lambda b, i: (b, 0)),
            pl.BlockSpec((seq, W_SB), lambda b, i: (b, 0)),
        ] + x_specs,
        out_specs=pl.BlockSpec((CHUNK, W_SB), lambda b, i: (b * nq + i, ycol)),
        out_shape=y_shape,
        scratch_shapes=[pltpu.VMEM((SB_HEADS, CHUNK, HEAD_DIM), F32), pltpu.VMEM((SB_HEADS, CHUNK, CHUNK), F32)],
        compiler_params=_params(("parallel", "arbitrary")),
        name="sb_prompt_attention",
    )(qn, kb, vb, *x_args)


def _sb_decode_kernel(pt_ref, q_ref, *refs, n_steps):
    G = DECODE_PAGES_PER_STEP
    k_refs, v_refs = refs[:G], refs[G:2 * G]
    o_ref, acc_ref, run_ref = refs[2 * G:]
    s = pl.program_id(1)

    @pl.when(s == 0)
    def _():
        acc_ref[...] = jnp.zeros_like(acc_ref)
        run_ref[...] = jnp.zeros_like(run_ref)

    P = PAGE_SIZE
    q = q_ref[0]
    head_row = lax.broadcasted_iota(jnp.int32, (SB_HEADS, P), 0)
    sfx = _suffix_matrix(P)
    head_rows = lambda ref, h: ref[pl.ds(h, P, stride=SB_HEADS), :].astype(BF16)

    scores = [[lax.dot_general(q, head_rows(k_refs[r], h), _NT, preferred_element_type=F32)
               for h in range(SB_HEADS)] for r in range(G)]
    logs = []
    for r in range(G):
        zz = scores[r][0]
        for h in range(1, SB_HEADS):
            zz = jnp.where(head_row == h, scores[r][h], zz)
        logs.append(_log_sigmoid_pair(zz * (HEAD_DIM ** -0.5)))
    sums = [_suffix_and_total(l1, sfx) for _, l1 in logs]
    run = run_ref[...]
    weights = []
    for r in range(G):
        weights.append(jnp.exp(logs[r][0] + sums[r][0] + run).astype(BF16))
        run = run + sums[r][1]
    run_ref[...] = run
    parts = [[jnp.dot(weights[r], head_rows(v_refs[r], h), preferred_element_type=F32)
              for h in range(SB_HEADS)] for r in range(G)]
    acc = acc_ref[...]
    for h in range(SB_HEADS):
        tot = parts[0][h]
        for r in range(1, G):
            tot = tot + parts[r][h]
        acc = acc + jnp.where(head_row == h, tot, 0.0)
    acc_ref[...] = acc

    @pl.when(s == n_steps - 1)
    def _():
        o_ref[0] = acc


def sb_decode_attention(qn, cache_k, cache_v, pages):
    nb, n_pages = pages.shape
    G = DECODE_PAGES_PER_STEP
    n_steps = n_pages // G
    page = lambda r: pl.BlockSpec((None, PAGE_SIZE * SB_HEADS, HEAD_DIM),
                                  lambda b, s, pt: (pt[b, n_pages - 1 - (s * G + r)], 0, 0))
    grid_spec = pltpu.PrefetchScalarGridSpec(
        num_scalar_prefetch=1,
        grid=(nb, n_steps),
        in_specs=[pl.BlockSpec((1, SB_HEADS, HEAD_DIM), lambda b, s, pt: (b, 0, 0))]
        + [page(r) for r in range(G)] * 2,
        out_specs=pl.BlockSpec((1, SB_HEADS, HEAD_DIM), lambda b, s, pt: (b, 0, 0)),
        scratch_shapes=[pltpu.VMEM((SB_HEADS, HEAD_DIM), F32), pltpu.VMEM((SB_HEADS, PAGE_SIZE), F32)],
    )
    out = pl.pallas_call(
        functools.partial(_sb_decode_kernel, n_steps=n_steps),
        grid_spec=grid_spec,
        out_shape=jax.ShapeDtypeStruct((nb, SB_HEADS, HEAD_DIM), F32),
        compiler_params=_params(("parallel", "arbitrary")),
        name="sb_decode_attention",
    )(pages, qn.reshape(nb, SB_HEADS, HEAD_DIM), *([cache_k] * G), *([cache_v] * G))
    return out.reshape(nb, W_SB).astype(BF16)


def _dense_in(l, xp, xs, norm_w, wa, wdt, wb, w_out):
    hp, hs = rmsnorm_bf16(xp, norm_w), rmsnorm_bf16(xs, norm_w)
    proj = functools.partial(matmul, hp, hs, tm=1024, tk=D_MODEL, w_transposed=True)
    pa, pa_s = proj(wa, tn=1024, name="in_proj_a")
    dt, dt_s = proj(wdt, tn=LANES, name="dt_proj")
    pb, pb_s, wo = proj(wb, tn=1024, casts=[("plain", w_out, l)], name="in_proj_b")
    return (pa, dt, pb), (pa_s, dt_s, pb_s), wo


def _dense_out(l, xp, xs, mix_p, mix_s, wo, norm2_w, w1, w_in, w_ff1, w_ff2):
    more = l + 1 < DEPTH
    xp, xs = matmul(mix_p, mix_s, wo, tm=1024, tn=512, tk=D_MODEL, epilogue="residual", residual=(xp, xs),
                    name="out_proj")
    hp, hs = rmsnorm_bf16(xp, norm2_w), rmsnorm_bf16(xs, norm2_w)
    fp, fs, w2, *w1_next = matmul(hp, hs, w1, tm=1024, tn=1024, tk=D_MODEL, out_dtype=BF16, epilogue="relu2",
                                  casts=[("plain", w_ff2, l)] + ([("plain", w_ff1, l + 1)] if more else []),
                                  name="ffn_up")
    xp, xs, *w_in_next = matmul(fp, fs, w2, tm=1024, tn=1024, tk=2048, epilogue="residual", residual=(xp, xs),
                                casts=[("w_in", w_in, l + 1)] if more else [], name="ffn_down")
    return xp, xs, w1_next, w_in_next


def kernel(x_prompt, x_sample, cache_k, cache_v, state_ssm, state_conv, state_pool, state_ret, page_table, norm1_w, w_in, conv_w, conv_b, dt_bias, a_log, d_skip, ssd_norm_w, pool_w, pool_scale, ret_gn_w, sb_q_norm, sb_k_norm, w_out, norm2_w, w_ff1, w_ff2):
    bp, seq, _ = x_prompt.shape
    bs = x_sample.shape[0]
    n_phys = cache_k.shape[1]
    past_len = page_table.shape[1] * PAGE_SIZE
    ncp = seq // CHUNK
    xp = x_prompt.reshape(bp * seq, D_MODEL)
    xs = x_sample.reshape(bs, D_MODEL)
    ck = cache_k.reshape(DEPTH * n_phys, PAGE_SIZE * SB_HEADS, HEAD_DIM)
    cv = cache_v.reshape(DEPTH * n_phys, PAGE_SIZE * SB_HEADS, HEAD_DIM)
    zeros = lambda *shape: jnp.zeros(shape, F32)
    outs = {name: [] for name in ("sp", "ss", "cp", "cs", "pp", "ps", "rp", "rs")}
    kv_p = kv_s = None

    w_in = jnp.swapaxes(w_in, 1, 2)
    wa, wdt, wb, w1 = convert_weights([("w_in", w_in, 0), ("plain", w_ff1, 0)])
    padded = lambda a: jnp.pad(a[:, None, :], ((0, 0), (0, CHUNK - 1), (0, 0))).reshape(bs * CHUNK, a.shape[1])
    first = lambda y: y.reshape(bs, CHUNK, -1)[:, 0, :]

    for l in range(DEPTH):
        ssd_w = (conv_w[l], conv_b[l], dt_bias[l], a_log[l], d_skip[l], ssd_norm_w[l])
        (pa, dt_raw, pb), (pa_s, dt_s, pb_s), wo = _dense_in(l, xp, xs, norm1_w[l], wa, wdt, wb, w_out)

        mix_p, cp, sp = ssd_mixer(pa, dt_raw, zeros(bp, 8, SSD_CONV_CH), zeros(bp, SSD_HEADS, 64, SSD_STATE), *ssd_w,
                                  nb=bp, nc=ncp, n_valid=CHUNK, mix="new")
        mix_p, pp = pool_mixer(pb, zeros(bp, POOL_BUF + 1, W_POOL), pool_w[l], pool_scale[l],
                               nb=bp, nt=seq // 256, tl=256, n_valid=256, start=0, mix=mix_p)
        mix_p, rp = retention_mixer(pb, zeros(bp, RET_HEADS, HEAD_DIM, HEAD_DIM), ret_gn_w[l],
                                    nb=bp, nc=ncp, n_valid=CHUNK, start=0, mix=mix_p)
        qn, kb, vb, kv_p = sb_prep(pb, sb_q_norm[l], sb_k_norm[l], l, kv_p)
        mix_p = sb_prompt_attention(qn, kb, vb, nb=bp, seq=seq, mix=mix_p)
        outs["sp"].append(sp); outs["cp"].append(cp); outs["pp"].append(pp); outs["rp"].append(rp)

        pa_sp, dt_sp, pb_sp = padded(pa_s), padded(dt_s), padded(pb_s)
        y_ssd, cs, ss = ssd_mixer(pa_sp, dt_sp, jnp.pad(state_conv[l], ((0, 0), (5, 0), (0, 0))), state_ssm[l], *ssd_w,
                                  nb=bs, nc=1, n_valid=1)
        y_pool, ps = pool_mixer(pb_sp, jnp.pad(state_pool[l], ((0, 0), (1, 0), (0, 0))), pool_w[l], pool_scale[l],
                                nb=bs, nt=1, tl=CHUNK, n_valid=1, start=past_len)
        y_ret, rs = retention_mixer(pb_sp, state_ret[l], ret_gn_w[l], nb=bs, nc=1, n_valid=1, start=past_len)
        qn, kb, vb, kv_s = sb_prep(pb_s, sb_q_norm[l], sb_k_norm[l], l, kv_s)
        y_sb = sb_decode_attention(qn, ck, cv, page_table + l * n_phys)
        mix_s = jnp.concatenate([first(y_ssd), first(y_pool), first(y_ret), y_sb], axis=1)
        outs["ss"].append(ss); outs["cs"].append(cs); outs["ps"].append(ps); outs["rs"].append(rs)

        xp, xs, w1_next, w_in_next = _dense_out(l, xp, xs, mix_p, mix_s, wo, norm2_w[l], w1, w_in, w_ff1, w_ff2)
        if l + 1 < DEPTH:
            (w1,), (wa, wdt, wb) = w1_next, w_in_next

    st = lambda name: jnp.stack(outs[name])
    kv = lambda a, nb, n: a.reshape(DEPTH, nb, n, SB_HEADS, HEAD_DIM)
    return (xp.reshape(bp, seq, D_MODEL), xs.reshape(bs, 1, D_MODEL),
            kv(kv_p[0], bp, seq), kv(kv_p[1], bp, seq), kv(kv_s[0], bs, 1), kv(kv_s[1], bs, 1),
            st("sp"), st("ss"), st("cp"), st("cs"), st("pp"), st("ps"), st("rp"), st("rs"))
```

```python
import functools
import math

import jax
import jax.numpy as jnp
from jax import lax
from jax.experimental import pallas as pl
from jax.experimental.pallas import tpu as pltpu

F32 = jnp.float32
BF16 = jnp.bfloat16

D_MODEL = 4096
DEPTH = 4
PAGE_SIZE = 128
W_SSD = W_POOL = W_RET = W_SB = 1024
SSD_HEADS = 16
SSD_GROUPS = 4
SSD_STATE = 128
SSD_GN = SSD_GROUPS * SSD_STATE
SSD_CONV = 4
SSD_CONV_CH = W_SSD + 2 * SSD_GN
POOL_WINDOWS = (2, 4, 8, 16)
POOL_GC = W_POOL // len(POOL_WINDOWS)
POOL_BUF = max(POOL_WINDOWS) - 1
RET_HEADS = 8
HEAD_DIM = 128
SB_HEADS = 8
ROPE_BASE = 10000.0
D_FF = 4 * D_MODEL
NORM_EPS = 1e-6
CHUNK = 128
DECODE_PAGES_PER_STEP = 8
PROJ_A = W_SSD * 2 + SSD_GN * 2
PROJ_B_COLS = ("u", "rq", "rk", "rv", "rg", "sq", "sk", "sv")
PROJ_B = 1024 * len(PROJ_B_COLS)
BF16_SUBLANES = 16
LANES = 128
NEG_BIG = -1e30
VMEM_LIMIT = 58 * 1024 * 1024

_NT = (((1,), (1,)), ((), ()))


def _params(sem):
    return pltpu.CompilerParams(dimension_semantics=sem, vmem_limit_bytes=VMEM_LIMIT)


def _silu(x):
    return x / (1.0 + jnp.exp(-x))


def _softplus(x):
    return jnp.maximum(x, 0.0) + jnp.log1p(jnp.exp(-jnp.abs(x)))


def _dot(a, b):
    return jnp.dot(a.astype(BF16), b.astype(BF16), preferred_element_type=F32)


def _dot_nt(a, b):
    return lax.dot_general(a.astype(BF16), b.astype(BF16), _NT, preferred_element_type=F32)


def _rmsnorm_kernel(x_ref, w_ref, o_ref):
    x = x_ref[...]
    ms = jnp.mean(x * x, axis=-1, keepdims=True)
    o_ref[...] = (x * lax.rsqrt(ms + NORM_EPS) * w_ref[...]).astype(o_ref.dtype)


def rmsnorm_bf16(x, w):
    m, d = x.shape
    tr = min(m, 512)
    return pl.pallas_call(
        _rmsnorm_kernel,
        grid=(m // tr,),
        in_specs=[pl.BlockSpec((tr, d), lambda i: (i, 0)), pl.BlockSpec((1, d), lambda i: (0, 0))],
        out_specs=pl.BlockSpec((tr, d), lambda i: (i, 0)),
        out_shape=jax.ShapeDtypeStruct((m, d), BF16),
        compiler_params=_params(("parallel",)),
        name="rmsnorm",
    )(x, w.reshape(1, d))


def _mm_kernel(*refs, nk, epilogue, casts, w_transposed, w_in_period):
    xb_ref, xs_ref, w_ref = refs[:3]
    n_res = 2 if epilogue == "residual" else 0
    rb_ref, rs_ref = refs[3:5] if n_res else (None, None)
    n_src = sum(_N_SRC[c] for c in casts)
    n_in = 3 + n_res + n_src
    src_refs = refs[3 + n_res:n_in]
    ob_ref, os_ref = refs[n_in:n_in + 2]
    dst_refs = refs[n_in + 2:]
    k = pl.program_id(2)
    flat = (pl.program_id(0) * pl.num_programs(1) + pl.program_id(1)) * nk + k
    _convert_blocks(casts, src_refs, dst_refs, w_in_fresh=flat % w_in_period == 0)

    def run(x_ref, r_ref, o_ref):
        if w_transposed:
            part = lax.dot_general(x_ref[...], w_ref[...], _NT, preferred_element_type=F32)
        else:
            part = jnp.dot(x_ref[...], w_ref[...], preferred_element_type=F32)
        if epilogue == "relu2":
            r = jnp.maximum(part, 0.0)
            o_ref[...] = (r * r).astype(o_ref.dtype)
        elif nk == 1:
            o_ref[...] = part if r_ref is None else part + r_ref[...]
        else:
            @pl.when(k == 0)
            def _():
                o_ref[...] = part if r_ref is None else part + r_ref[...]

            @pl.when(k > 0)
            def _():
                o_ref[...] += part

    run(xb_ref, rb_ref, ob_ref)

    @pl.when(pl.program_id(1) == 0)
    def _():
        run(xs_ref, rs_ref, os_ref)


W_IN_STEPS = 64
_A_ROWS = PROJ_A // W_IN_STEPS
_B_ROWS = PROJ_B // W_IN_STEPS
_DT_BLOCKS = LANES // BF16_SUBLANES
_N_SRC = {"plain": 1, "w_in": 4}
_N_DST = {"plain": 1, "w_in": 3}


def _convert_blocks(kinds, src_refs, dst_refs, w_in_fresh=True):
    src_refs, dst_refs = list(src_refs), list(dst_refs)
    for kind in kinds:
        if kind == "w_in":
            src_a, src_dt, src_b, src_tail = (src_refs.pop(0) for _ in range(4))
            dst_a, dst_dt, dst_b = (dst_refs.pop(0) for _ in range(3))

            def split(src_a=src_a, src_dt=src_dt, src_b=src_b, src_tail=src_tail, dst_a=dst_a, dst_dt=dst_dt, dst_b=dst_b):
                dst_a[...] = src_a[...].astype(BF16)
                dst_dt[...] = src_dt[...].astype(BF16)
                dst_b[:_B_ROWS - SSD_HEADS, :] = src_b[SSD_HEADS:, :].astype(BF16)
                dst_b[_B_ROWS - SSD_HEADS:, :] = src_tail[...].astype(BF16)

            split() if w_in_fresh is True else pl.when(w_in_fresh)(split)
        else:
            dst_refs.pop(0)[...] = src_refs.pop(0)[...].astype(BF16)


def _src_spec(n, cols, layer, block):
    return pl.BlockSpec((None, n, cols), lambda *g: (layer, block(*g), 0))


def _dst_spec(n, cols, block):
    return pl.BlockSpec((n, cols), lambda *g: (block(*g), 0))


def _convert_specs(casts, total, flat):
    in_specs, args, out_specs, out_shape = [], [], [], []
    for kind, src, layer in casts:
        rows, cols = src.shape[1:]
        shape = lambda n: jax.ShapeDtypeStruct((n, cols), BF16)
        if kind == "w_in":
            assert rows == PROJ_A + SSD_HEADS + PROJ_B and SSD_HEADS == BF16_SUBLANES and total % W_IN_STEPS == 0
            coarse = lambda *g: flat(*g) // (total // W_IN_STEPS)
            dt_block = lambda *g: jnp.minimum(coarse(*g), _DT_BLOCKS - 1)
            in_specs += [
                _src_spec(_A_ROWS, cols, layer, coarse),
                _src_spec(BF16_SUBLANES, cols, layer, lambda *g: PROJ_A // BF16_SUBLANES + dt_block(*g)),
                _src_spec(_B_ROWS, cols, layer, lambda *g: PROJ_A // _B_ROWS + coarse(*g)),
                _src_spec(SSD_HEADS, cols, layer,
                          lambda *g: (PROJ_A + _B_ROWS) // SSD_HEADS + coarse(*g) * (_B_ROWS // SSD_HEADS)),
            ]
            args += [src] * 4
            out_specs += [_dst_spec(_A_ROWS, cols, coarse), _dst_spec(BF16_SUBLANES, cols, dt_block),
                          _dst_spec(_B_ROWS, cols, coarse)]
            out_shape += [shape(PROJ_A), shape(LANES), shape(PROJ_B)]
        else:
            rs = rows // total
            assert rs * total == rows and rs % BF16_SUBLANES == 0
            in_specs.append(_src_spec(rs, cols, layer, flat))
            args.append(src)
            out_specs.append(_dst_spec(rs, cols, flat))
            out_shape.append(shape(rows))
    return in_specs, args, out_specs, out_shape


def convert_weights(casts):
    kinds = tuple(c[0] for c in casts)
    in_specs, args, out_specs, out_shape = _convert_specs(casts, W_IN_STEPS, lambda i: i)
    n_src = sum(_N_SRC[c] for c in kinds)

    def body(*refs):
        _convert_blocks(kinds, refs[:n_src], refs[n_src:])

    return pl.pallas_call(body, grid=(W_IN_STEPS,), in_specs=in_specs, out_specs=out_specs, out_shape=out_shape,
                          compiler_params=_params(("arbitrary",)), name="convert_weights")(*args)


def matmul(xb, xs, w, *, tm, tn, tk, out_dtype=F32, epilogue=None, residual=None, casts=(), w_transposed=False,
           name="matmul"):
    m, kdim = xb.shape
    ms = xs.shape[0]
    n = w.shape[0] if w_transposed else w.shape[1]
    tm, tn, tk = min(tm, m), min(tn, n), min(tk, kdim)
    nk = kdim // tk
    ni = m // tm
    n_steps = (n // tn) * ni * nk
    step = lambda j, i, k: (j * ni + i) * nk + k
    w_spec = pl.BlockSpec((tn, tk), lambda j, i, k: (j, k)) if w_transposed else pl.BlockSpec((tk, tn), lambda j, i, k: (k, j))
    in_specs = [
        pl.BlockSpec((tm, tk), lambda j, i, k: (i, k)),
        pl.BlockSpec((ms, tk), lambda j, i, k: (0, k)),
        w_spec,
    ]
    args = [xb, xs, w]
    out_specs = [pl.BlockSpec((tm, tn), lambda j, i, k: (i, j)), pl.BlockSpec((ms, tn), lambda j, i, k: (0, j))]
    out_shape = [jax.ShapeDtypeStruct((m, n), out_dtype), jax.ShapeDtypeStruct((ms, n), out_dtype)]
    if epilogue == "residual":
        in_specs += [pl.BlockSpec((tm, tn), lambda j, i, k: (i, j)), pl.BlockSpec((ms, tn), lambda j, i, k: (0, j))]
        args += list(residual)
    assert nk == 1 or out_dtype == F32
    c_in, c_args, c_out, c_shape = _convert_specs(casts, n_steps, step)
    return pl.pallas_call(
        functools.partial(_mm_kernel, nk=nk, epilogue=epilogue, casts=tuple(c[0] for c in casts),
                          w_transposed=w_transposed, w_in_period=max(n_steps // W_IN_STEPS, 1)),
        grid=(n // tn, ni, nk),
        in_specs=in_specs + c_in,
        out_specs=out_specs + c_out,
        out_shape=out_shape + c_shape,
        compiler_params=_params(("arbitrary", "arbitrary", "arbitrary")),
        name=name,
    )(*args, *c_args)


def _ssd_kernel(xs_ref, z_ref, bm_ref, cm_ref, dt_ref, cs_ref, s0_ref, cw_ref, cb_ref, dtb_ref, alog_ref,
                dsk_ref, nw_ref, y_ref, cnew_ref, snew_ref, ext_ref, s_ref, *, nc, n_valid):
    c = pl.program_id(1)
    C = CHUNK

    @pl.when(c == 0)
    def _():
        ext_ref[0:8, :] = cs_ref[0]
        s_ref[...] = s0_ref[0]

    ext_ref[8:8 + C, 0:W_SSD] = xs_ref[...]
    ext_ref[8:8 + C, W_SSD:W_SSD + SSD_GN] = bm_ref[...]
    ext_ref[8:8 + C, W_SSD + SSD_GN:] = cm_ref[...]
    cw = cw_ref[...]
    conv = cb_ref[...] + ext_ref[8:8 + C, :] * cw[3:4, :]
    for tap in range(SSD_CONV - 1):
        conv = conv + ext_ref[5 + tap:5 + tap + C, :] * cw[tap:tap + 1, :]

    @pl.when(c == nc - 1)
    def _():
        cnew_ref[0] = ext_ref[5 + n_valid:8 + n_valid, :]

    ext_ref[0:8, :] = ext_ref[C:C + 8, :]

    xbc = _silu(conv)
    x = xbc[:, :W_SSD]
    bmat = xbc[:, W_SSD:W_SSD + SSD_GN]
    cmat = xbc[:, W_SSD + SSD_GN:]

    dt = _softplus(dt_ref[...] + dtb_ref[...])
    if n_valid < C:
        dt = jnp.where(lax.broadcasted_iota(jnp.int32, (C, LANES), 0) < n_valid, dt, 0.0)
    da = dt * (-jnp.exp(alog_ref[...]))
    ri = lax.broadcasted_iota(jnp.int32, (C, C), 0)
    ci = lax.broadcasted_iota(jnp.int32, (C, C), 1)
    tril = ri >= ci
    cum = jnp.dot(tril.astype(F32), da, precision=lax.Precision.HIGHEST, preferred_element_type=F32)
    cum_t = cum.T
    end = cum[C - 1:C, :]
    e_end = jnp.exp(end)
    lo = lax.broadcasted_iota(jnp.int32, (C, LANES), 1) < 64
    row_lo = lax.broadcasted_iota(jnp.int32, (C, LANES), 0) < 64
    dsk = dsk_ref[...]

    def pick(v, h0):
        return jnp.where(lo, v[:, h0:h0 + 1], v[:, h0 + 1:h0 + 2])

    ys = []
    for g in range(SSD_GROUPS):
        bg = bmat[:, g * SSD_STATE:(g + 1) * SSD_STATE].astype(BF16)
        cg = cmat[:, g * SSD_STATE:(g + 1) * SSD_STATE].astype(BF16)
        cb = _dot_nt(cg, bg)
        for pr in range(2):
            p = 2 * g + pr
            h0 = 2 * p
            xp = x[:, p * LANES:(p + 1) * LANES]
            xdt = xp * pick(dt, h0)
            y_in = jnp.zeros((C, LANES), F32)
            for h, keep in ((h0, lo), (h0 + 1, jnp.logical_not(lo))):
                seg = cum[:, h:h + 1] - cum_t[h:h + 1, :]
                decay = jnp.exp(jnp.where(tril, seg, NEG_BIG))
                y_in = y_in + _dot(cb * decay, jnp.where(keep, xdt, 0.0))
            sp = s_ref[p]
            y_off = _dot_nt(cg, sp) * jnp.exp(pick(cum, h0))
            ys.append(y_in + y_off + jnp.where(lo[0:1], dsk[:, h0:h0 + 1], dsk[:, h0 + 1:h0 + 2]) * xp)
            w_end = jnp.exp(pick(jnp.broadcast_to(end, (C, LANES)) - cum, h0))
            s_scale = jnp.where(row_lo, e_end[:, h0:h0 + 1], e_end[:, h0 + 1:h0 + 2])
            s_ref[p] = sp * s_scale + _dot((xdt * w_end).T, bg)

    y = jnp.concatenate(ys, axis=1) * _silu(z_ref[...])
    ms = jnp.mean(y * y, axis=-1, keepdims=True)
    y_ref[...] = (y * lax.rsqrt(ms + NORM_EPS) * nw_ref[...]).astype(y_ref.dtype)

    @pl.when(c == nc - 1)
    def _():
        snew_ref[0] = s_ref[...]


def _mix_target(rows, mix, col, kernel_fn, n_in):
    if mix is None:
        return kernel_fn, jax.ShapeDtypeStruct((rows, 1024), BF16), 0, {}, [], []
    shape = jax.ShapeDtypeStruct((rows, D_MODEL), BF16)
    if isinstance(mix, str):
        return kernel_fn, shape, col, {}, [], []
    skipping = lambda *refs: kernel_fn(*refs[:n_in], *refs[n_in + 1:])
    return skipping, shape, col, {"input_output_aliases": {n_in: 0}}, [pl.BlockSpec(memory_space=pl.ANY)], [mix]


def ssd_mixer(proj, dt_raw, conv_state8, ssm_state, conv_w, conv_b, dt_bias, a_log, d_skip, norm_w, *, nb, nc, n_valid,
              mix=None):
    assert n_valid == CHUNK or nc == 1
    rows = nb * nc * CHUNK
    pad = lambda v: jnp.pad(v, (0, LANES - SSD_HEADS)).reshape(1, LANES)
    row = lambda b, c: b * nc + c
    full = lambda shape: pl.BlockSpec(shape, lambda b, c: (0,) * len(shape))
    n_pair = SSD_HEADS // 2
    kern, y_shape, ycol, alias, x_specs, x_args = _mix_target(
        rows, mix, 0, functools.partial(_ssd_kernel, nc=nc, n_valid=n_valid), 13)
    y, conv_new, ssm_new = pl.pallas_call(
        kern,
        grid=(nb, nc),
        **alias,
        in_specs=[
            pl.BlockSpec((CHUNK, W_SSD), lambda b, c: (row(b, c), 0)),
            pl.BlockSpec((CHUNK, W_SSD), lambda b, c: (row(b, c), 1)),
            pl.BlockSpec((CHUNK, SSD_GN), lambda b, c: (row(b, c), 4)),
            pl.BlockSpec((CHUNK, SSD_GN), lambda b, c: (row(b, c), 5)),
            pl.BlockSpec((CHUNK, LANES), lambda b, c: (row(b, c), 0)),
            pl.BlockSpec((1, 8, SSD_CONV_CH), lambda b, c: (b, 0, 0)),
            pl.BlockSpec((1, n_pair, LANES, SSD_STATE), lambda b, c: (b, 0, 0, 0)),
            full((SSD_CONV, SSD_CONV_CH)),
            full((1, SSD_CONV_CH)),
            full((1, LANES)),
            full((1, LANES)),
            full((1, LANES)),
            full((1, W_SSD)),
        ] + x_specs,
        out_specs=[
            pl.BlockSpec((CHUNK, W_SSD), lambda b, c: (row(b, c), ycol)),
            pl.BlockSpec((1, SSD_CONV - 1, SSD_CONV_CH), lambda b, c: (b, 0, 0)),
            pl.BlockSpec((1, n_pair, LANES, SSD_STATE), lambda b, c: (b, 0, 0, 0)),
        ],
        out_shape=[
            y_shape,
            jax.ShapeDtypeStruct((nb, SSD_CONV - 1, SSD_CONV_CH), F32),
            jax.ShapeDtypeStruct((nb, n_pair, LANES, SSD_STATE), F32),
        ],
        scratch_shapes=[pltpu.VMEM((CHUNK + 8, SSD_CONV_CH), F32), pltpu.VMEM((n_pair, LANES, SSD_STATE), F32)],
        compiler_params=_params(("parallel", "arbitrary")),
        name="ssd_mixer",
    )(proj, proj, proj, proj, dt_raw, conv_state8, ssm_state.reshape(nb, n_pair, LANES, SSD_STATE),
      conv_w, conv_b.reshape(1, -1), pad(dt_bias), pad(a_log), pad(d_skip), norm_w.reshape(1, -1), *x_args)
    return y, conv_new, ssm_new.reshape(nb, SSD_HEADS, 64, SSD_STATE)


def _pool_kernel(u_ref, buf_ref, pw_ref, ps_ref, y_ref, bnew_ref, ext_ref, *, nt, tl, n_valid, start):
    t = pl.program_id(1)
    H = POOL_BUF + 1

    @pl.when(t == 0)
    def _():
        ext_ref[0:H, :] = buf_ref[0]

    ext_ref[H:H + tl, :] = u_ref[...]
    pos = start + t * tl + lax.broadcasted_iota(jnp.int32, (tl, 1), 0)
    outs = []
    for g, w in enumerate(POOL_WINDOWS):
        cols = slice(g * POOL_GC, (g + 1) * POOL_GC)
        u = ext_ref[H:H + tl, cols]
        ssum = u
        for k in range(1, w):
            ssum = ssum + ext_ref[H - k:H - k + tl, cols]
        cnt = jnp.minimum(pos + 1, w).astype(F32)
        d = ssum / cnt - u
        outs.append(_dot(d, pw_ref[g]))
    y_ref[...] = (jnp.concatenate(outs, axis=1) * ps_ref[...]).astype(y_ref.dtype)

    @pl.when(t == nt - 1)
    def _():
        bnew_ref[0] = ext_ref[1 + n_valid:H + n_valid, :]

    ext_ref[0:H, :] = ext_ref[tl:tl + H, :]


def pool_mixer(proj, buf16, pool_w, pool_scale, *, nb, nt, tl, n_valid, start, mix=None):
    assert n_valid == tl or nt == 1
    rows = nb * nt * tl
    ublk = PROJ_B_COLS.index("u")
    kern, y_shape, ycol, alias, x_specs, x_args = _mix_target(
        rows, mix, 1, functools.partial(_pool_kernel, nt=nt, tl=tl, n_valid=n_valid, start=start), 4)
    return pl.pallas_call(
        kern,
        grid=(nb, nt),
        **alias,
        in_specs=[
            pl.BlockSpec((tl, W_POOL), lambda b, t: (b * nt + t, ublk)),
            pl.BlockSpec((1, POOL_BUF + 1, W_POOL), lambda b, t: (b, 0, 0)),
            pl.BlockSpec((len(POOL_WINDOWS), POOL_GC, POOL_GC), lambda b, t: (0, 0, 0)),
            pl.BlockSpec((1, W_POOL), lambda b, t: (0, 0)),
        ] + x_specs,
        out_specs=[
            pl.BlockSpec((tl, W_POOL), lambda b, t: (b * nt + t, ycol)),
            pl.BlockSpec((1, POOL_BUF, W_POOL), lambda b, t: (b, 0, 0)),
        ],
        out_shape=[y_shape, jax.ShapeDtypeStruct((nb, POOL_BUF, W_POOL), F32)],
        scratch_shapes=[pltpu.VMEM((tl + POOL_BUF + 1, W_POOL), F32)],
        compiler_params=_params(("parallel", "arbitrary")),
        name="pool_mixer",
    )(proj, buf16, pool_w.astype(BF16), pool_scale.reshape(1, -1), *x_args)


def _ret_kernel(q_ref, k_ref, v_ref, g_ref, cos_ref, sin_ref, s0_ref, gnw_ref, y_ref, snew_ref, s_ref, *, nc, n_valid):
    c = pl.program_id(1)
    C = CHUNK

    @pl.when(c == 0)
    def _():
        s_ref[...] = s0_ref[0]

    cos_f = cos_ref[...]
    sin_f = sin_ref[...]
    ri = lax.broadcasted_iota(jnp.int32, (C, C), 0)
    ci = lax.broadcasted_iota(jnp.int32, (C, C), 1)
    causal = ri >= ci
    diff = jnp.where(causal, ri - ci, 0).astype(F32)
    rowi = lax.broadcasted_iota(jnp.int32, (C, 1), 0)
    rowf = rowi.astype(F32)
    for h in range(RET_HEADS):
        lg = math.log1p(-(2.0 ** (-5 - h)))
        sl = slice(h * HEAD_DIM, (h + 1) * HEAD_DIM)
        qh, kh, vh = q_ref[:, sl], k_ref[:, sl], v_ref[:, sl]
        qr = (qh * cos_f + pltpu.roll(qh, HEAD_DIM // 2, 1) * sin_f) * (HEAD_DIM ** -0.5)
        kr = kh * cos_f + pltpu.roll(kh, HEAD_DIM // 2, 1) * sin_f
        intra = jnp.where(causal, jnp.exp(lg * diff), 0.0)
        s = _dot_nt(qr, kr) * intra
        q_dec = jnp.exp(lg * (rowf + 1.0))
        k_dec = jnp.where(rowi < n_valid, jnp.exp(lg * (n_valid - 1.0 - rowf)), 0.0)
        sh = s_ref[h]
        y = _dot(s, vh) + _dot(qr * q_dec, sh)
        s_ref[h] = sh * math.exp(lg * n_valid) + _dot((kr * k_dec).T, vh)
        mu = jnp.mean(y, axis=-1, keepdims=True)
        yc = y - mu
        var = jnp.mean(yc * yc, axis=-1, keepdims=True)
        yn = yc * lax.rsqrt(var + NORM_EPS) * gnw_ref[:, sl]
        y_ref[:, sl] = (_silu(g_ref[:, sl]) * yn).astype(y_ref.dtype)

    @pl.when(c == nc - 1)
    def _():
        snew_ref[0] = s_ref[...]


def _rope_tables(start, n):
    half = HEAD_DIM // 2
    inv = ROPE_BASE ** (-jnp.arange(half, dtype=F32) / half)
    ang = (start + jnp.arange(n)).astype(F32)[:, None] * inv[None, :]
    cos, sin = jnp.cos(ang), jnp.sin(ang)
    return jnp.concatenate([cos, cos], axis=1), jnp.concatenate([-sin, sin], axis=1)


def retention_mixer(proj, ret_state, gn_w, *, nb, nc, n_valid, start, mix=None):
    assert n_valid == CHUNK or nc == 1
    rows = nb * nc * CHUNK
    cos_f, sin_f = _rope_tables(start, nc * CHUNK)
    col = lambda j: pl.BlockSpec((CHUNK, W_RET), lambda b, c: (b * nc + c, j))
    kern, y_shape, ycol, alias, x_specs, x_args = _mix_target(
        rows, mix, 2, functools.partial(_ret_kernel, nc=nc, n_valid=n_valid), 8)
    return pl.pallas_call(
        kern,
        grid=(nb, nc),
        **alias,
        in_specs=[
            *[col(PROJ_B_COLS.index(name)) for name in ("rq", "rk", "rv", "rg")],
            pl.BlockSpec((CHUNK, HEAD_DIM), lambda b, c: (c, 0)),
            pl.BlockSpec((CHUNK, HEAD_DIM), lambda b, c: (c, 0)),
            pl.BlockSpec((1, RET_HEADS, HEAD_DIM, HEAD_DIM), lambda b, c: (b, 0, 0, 0)),
            pl.BlockSpec((1, W_RET), lambda b, c: (0, 0)),
        ] + x_specs,
        out_specs=[
            pl.BlockSpec((CHUNK, W_RET), lambda b, c: (b * nc + c, ycol)),
            pl.BlockSpec((1, RET_HEADS, HEAD_DIM, HEAD_DIM), lambda b, c: (b, 0, 0, 0)),
        ],
        out_shape=[y_shape, jax.ShapeDtypeStruct((nb, RET_HEADS, HEAD_DIM, HEAD_DIM), F32)],
        scratch_shapes=[pltpu.VMEM((RET_HEADS, HEAD_DIM, HEAD_DIM), F32)],
        compiler_params=_params(("parallel", "arbitrary")),
        name="retention_mixer",
    )(proj, proj, proj, proj, cos_f, sin_f, ret_state, gn_w.reshape(1, -1), *x_args)


def _sb_prep_kernel(q_ref, k_ref, v_ref, qw_ref, kw_ref, qn_ref, kn_ref, kb_ref, vf_ref, vb_ref):
    for h in range(SB_HEADS):
        sl = slice(h * HEAD_DIM, (h + 1) * HEAD_DIM)
        q = q_ref[:, sl]
        k = k_ref[:, sl]
        qn = q * lax.rsqrt(jnp.mean(q * q, axis=-1, keepdims=True) + NORM_EPS) * qw_ref[...]
        kn = k * lax.rsqrt(jnp.mean(k * k, axis=-1, keepdims=True) + NORM_EPS) * kw_ref[...]
        qn_ref[:, sl] = qn.astype(BF16)
        kn_ref[:, sl] = kn
        kb_ref[:, sl] = kn.astype(BF16)
    v = v_ref[...]
    vf_ref[...] = v
    vb_ref[...] = v.astype(BF16)


def sb_prep(proj, q_norm, k_norm, layer, kv_stacks=None):
    m = proj.shape[0]
    tr = min(m, 256)
    col = lambda j: pl.BlockSpec((tr, W_SB), lambda i: (i, j))
    out = pl.BlockSpec((tr, W_SB), lambda i: (i, 0))
    slot = pl.BlockSpec((None, tr, W_SB), lambda i: (layer, i, 0))
    sds = lambda dt: jax.ShapeDtypeStruct((m, W_SB), dt)
    stack = jax.ShapeDtypeStruct((DEPTH, m, W_SB), F32)
    n_in = 5
    kern, alias, x_specs, x_args = _sb_prep_kernel, {}, [], []
    if kv_stacks is not None:
        kern = lambda *refs: _sb_prep_kernel(*refs[:n_in], *refs[n_in + 2:])
        alias = {"input_output_aliases": {n_in: 1, n_in + 1: 3}}
        x_specs, x_args = [pl.BlockSpec(memory_space=pl.ANY)] * 2, list(kv_stacks)
    qn, k_stack, kb, v_stack, vb = pl.pallas_call(
        kern,
        grid=(m // tr,),
        **alias,
        in_specs=[*[col(PROJ_B_COLS.index(name)) for name in ("sq", "sk", "sv")],
                  pl.BlockSpec((1, HEAD_DIM), lambda i: (0, 0)),
                  pl.BlockSpec((1, HEAD_DIM), lambda i: (0, 0))] + x_specs,
        out_specs=[out, slot, out, slot, out],
        out_shape=[sds(BF16), stack, sds(BF16), stack, sds(BF16)],
        compiler_params=_params(("parallel",)),
        name="sb_prep",
    )(proj, proj, proj, q_norm.reshape(1, -1), k_norm.reshape(1, -1), *x_args)
    return qn, kb, vb, (k_stack, v_stack)


def _log_sigmoid_pair(zz):
    lb = jnp.minimum(zz, 0.0) - jnp.log(1.0 + jnp.exp(-jnp.abs(zz)))
    return lb, lb - zz


def _suffix_matrix(t):
    row = lax.broadcasted_iota(jnp.int32, (t, 2 * t), 0)
    col = lax.broadcasted_iota(jnp.int32, (t, 2 * t), 1)
    return jnp.logical_or(col >= t, row > col).astype(BF16)


def _suffix_and_total(l1, suffix_matrix):
    t = l1.shape[1]
    hi = l1.astype(BF16)
    lo = (l1 - hi.astype(F32)).astype(BF16)
    both = (jnp.dot(hi, suffix_matrix, preferred_element_type=F32)
            + jnp.dot(lo, suffix_matrix, preferred_element_type=F32))
    return both[:, :t], both[:, t:]


def _sb_prompt_kernel(q_ref, k_ref, v_ref, o_ref, acc_ref, run_ref):
    qi = pl.program_id(1)
    T = CHUNK
    ri = lax.broadcasted_iota(jnp.int32, (T, T), 0)
    ci = lax.broadcasted_iota(jnp.int32, (T, T), 1)
    allowed = ci < ri
    sfx = _suffix_matrix(T)
    scale = HEAD_DIM ** -0.5

    def tiles_step(kblks, diagonal):
        offs = [pl.multiple_of(kb * T, T) for kb in kblks]
        tiles, heads = range(len(kblks)), range(SB_HEADS)
        sl = lambda h: slice(h * HEAD_DIM, (h + 1) * HEAD_DIM)
        zz = [[lax.dot_general(q_ref[:, sl(h)], k_ref[pl.ds(offs[t], T), sl(h)], _NT, preferred_element_type=F32)
               for h in heads] for t in tiles]
        logs = [[_log_sigmoid_pair(zz[t][h] * scale) for h in heads] for t in tiles]
        l1 = [[jnp.where(allowed, logs[t][h][1], 0.0) if diagonal else logs[t][h][1] for h in heads] for t in tiles]
        sums = [[_suffix_and_total(l1[t][h], sfx) for h in heads] for t in tiles]
        weights = [[None] * SB_HEADS for _ in tiles]
        for h in heads:
            run = None if diagonal else run_ref[h]
            for t in tiles:
                log_a = logs[t][h][0] + sums[t][h][0]
                a = jnp.exp(log_a if run is None else log_a + run)
                weights[t][h] = (jnp.where(allowed, a, 0.0) if diagonal else a).astype(BF16)
                run = sums[t][h][1] if run is None else run + sums[t][h][1]
            run_ref[h] = run
        parts = [[jnp.dot(weights[t][h], v_ref[pl.ds(offs[t], T), sl(h)], preferred_element_type=F32)
                  for h in heads] for t in tiles]
        for h in heads:
            tot = parts[0][h]
            for t in tiles[1:]:
                tot = tot + parts[t][h]
            acc_ref[h] = tot if diagonal else acc_ref[h] + tot

    tiles_step([qi], True)

    def body(t, carry):
        tiles_step([qi - 1 - 4 * t - u for u in range(4)], False)
        return carry

    lax.fori_loop(0, qi // 4, body, 0)
    rest = qi % 4

    @pl.when(rest >= 2)
    def _():
        tiles_step([rest - 1, rest - 2], False)

    @pl.when(rest % 2 == 1)
    def _():
        tiles_step([0], False)
    for h in range(SB_HEADS):
        o_ref[:, h * HEAD_DIM:(h + 1) * HEAD_DIM] = acc_ref[h].astype(o_ref.dtype)


def sb_prompt_attention(qn, kb, vb, *, nb, seq, mix=None):
    nq = seq // CHUNK
    kern, y_shape, ycol, alias, x_specs, x_args = _mix_target(nb * seq, mix, 3, _sb_prompt_kernel, 3)
    return pl.pallas_call(
        kern,
        grid=(nb, nq),
        **alias,
        in_specs=[
            pl.BlockSpec((CHUNK, W_SB), lambda b, i: (b * nq + i, 0)),
            pl.BlockSpec((seq, W_SB), lambda b, i: (b, 0)),
            pl.BlockSpec((seq, W_SB), lambda b, i: (b, 0)),
        ] + x_specs,
        out_specs=pl.BlockSpec((CHUNK, W_SB), lambda b, i: (b * nq + i, ycol)),
        out_shape=y_shape,
        scratch_shapes=[pltpu.VMEM((SB_HEADS, CHUNK, HEAD_DIM), F32), pltpu.VMEM((SB_HEADS, CHUNK, CHUNK), F32)],
        compiler_params=_params(("parallel", "arbitrary")),
        name="sb_prompt_attention",
    )(qn, kb, vb, *x_args)


def _sb_decode_kernel(pt_ref, q_ref, *refs, n_steps):
    G = DECODE_PAGES_PER_STEP
    k_refs, v_refs = refs[:G], refs[G:2 * G]
    o_ref, acc_ref, run_ref = refs[2 * G:]
    s = pl.program_id(1)

    @pl.when(s == 0)
    def _():
        acc_ref[...] = jnp.zeros_like(acc_ref)
        run_ref[...] = jnp.zeros_like(run_ref)

    P = PAGE_SIZE
    q = q_ref[0]
    head_row = lax.broadcasted_iota(jnp.int32, (SB_HEADS, P), 0)
    sfx = _suffix_matrix(P)
    head_rows = lambda ref, h: ref[pl.ds(h, P, stride=SB_HEADS), :].astype(BF16)

    scores = [[lax.dot_general(q, head_rows(k_refs[r], h), _NT, preferred_element_type=F32)
               for h in range(SB_HEADS)] for r in range(G)]
    logs = []
    for r in range(G):
        zz = scores[r][0]
        for h in range(1, SB_HEADS):
            zz = jnp.where(head_row == h, scores[r][h], zz)
        logs.append(_log_sigmoid_pair(zz * (HEAD_DIM ** -0.5)))
    sums = [_suffix_and_total(l1, sfx) for _, l1 in logs]
    run = run_ref[...]
    weights = []
    for r in range(G):
        weights.append(jnp.exp(logs[r][0] + sums[r][0] + run).astype(BF16))
        run = run + sums[r][1]
    run_ref[...] = run
    parts = [[jnp.dot(weights[r], head_rows(v_refs[r], h), preferred_element_type=F32)
              for h in range(SB_HEADS)] for r in range(G)]
    acc = acc_ref[...]
    for h in range(SB_HEADS):
        tot = parts[0][h]
        for r in range(1, G):
            tot = tot + parts[r][h]
        acc = acc + jnp.where(head_row == h, tot, 0.0)
    acc_ref[...] = acc

    @pl.when(s == n_steps - 1)
    def _():
        o_ref[0] = acc


def sb_decode_attention(qn, cache_k, cache_v, pages):
    nb, n_pages = pages.shape
    G = DECODE_PAGES_PER_STEP
    n_steps = n_pages // G
    page = lambda r: pl.BlockSpec((None, PAGE_SIZE * SB_HEADS, HEAD_DIM),
                                  lambda b, s, pt: (pt[b, n_pages - 1 - (s * G + r)], 0, 0))
    grid_spec = pltpu.PrefetchScalarGridSpec(
        num_scalar_prefetch=1,
        grid=(nb, n_steps),
        in_specs=[pl.BlockSpec((1, SB_HEADS, HEAD_DIM), lambda b, s, pt: (b, 0, 0))]
        + [page(r) for r in range(G)] * 2,
        out_specs=pl.BlockSpec((1, SB_HEADS, HEAD_DIM), lambda b, s, pt: (b, 0, 0)),
        scratch_shapes=[pltpu.VMEM((SB_HEADS, HEAD_DIM), F32), pltpu.VMEM((SB_HEADS, PAGE_SIZE), F32)],
    )
    out = pl.pallas_call(
        functools.partial(_sb_decode_kernel, n_steps=n_steps),
        grid_spec=grid_spec,
        out_shape=jax.ShapeDtypeStruct((nb, SB_HEADS, HEAD_DIM), F32),
        compiler_params=_params(("parallel", "arbitrary")),
        name="sb_decode_attention",
    )(pages, qn.reshape(nb, SB_HEADS, HEAD_DIM), *([cache_k] * G), *([cache_v] * G))
    return out.reshape(nb, W_SB).astype(BF16)


def _dense_in(l, xp, xs, norm_w, wa, wdt, wb, w_out):
    hp, hs = rmsnorm_bf16(xp, norm_w), rmsnorm_bf16(xs, norm_w)
    proj = functools.partial(matmul, hp, hs, tm=1024, tk=D_MODEL, w_transposed=True)
    pa, pa_s = proj(wa, tn=1024, name="in_proj_a")
    dt, dt_s = proj(wdt, tn=LANES, name="dt_proj")
    pb, pb_s, wo = proj(wb, tn=1024, casts=[("plain", w_out, l)], name="in_proj_b")
    return (pa, dt, pb), (pa_s, dt_s, pb_s), wo


def _dense_out(l, xp, xs, mix_p, mix_s, wo, norm2_w, w1, w_in, w_ff1, w_ff2):
    more = l + 1 < DEPTH
    xp, xs = matmul(mix_p, mix_s, wo, tm=1024, tn=1024, tk=D_MODEL, epilogue="residual", residual=(xp, xs),
                    name="out_proj")
    hp, hs = rmsnorm_bf16(xp, norm2_w), rmsnorm_bf16(xs, norm2_w)
    fp, fs, w2, *w1_next = matmul(hp, hs, w1, tm=1024, tn=1024, tk=D_MODEL, out_dtype=BF16, epilogue="relu2",
                                  casts=[("plain", w_ff2, l)] + ([("plain", w_ff1, l + 1)] if more else []),
                                  name="ffn_up")
    xp, xs, *w_in_next = matmul(fp, fs, w2, tm=1024, tn=1024, tk=2048, epilogue="residual", residual=(xp, xs),
                                casts=[("w_in", w_in, l + 1)] if more else [], name="ffn_down")
    return xp, xs, w1_next, w_in_next


def kernel(x_prompt, x_sample, cache_k, cache_v, state_ssm, state_conv, state_pool, state_ret, page_table, norm1_w, w_in, conv_w, conv_b, dt_bias, a_log, d_skip, ssd_norm_w, pool_w, pool_scale, ret_gn_w, sb_q_norm, sb_k_norm, w_out, norm2_w, w_ff1, w_ff2):
    bp, seq, _ = x_prompt.shape
    bs = x_sample.shape[0]
    n_phys = cache_k.shape[1]
    past_len = page_table.shape[1] * PAGE_SIZE
    ncp = seq // CHUNK
    xp = x_prompt.reshape(bp * seq, D_MODEL)
    xs = x_sample.reshape(bs, D_MODEL)
    ck = cache_k.reshape(DEPTH * n_phys, PAGE_SIZE * SB_HEADS, HEAD_DIM)
    cv = cache_v.reshape(DEPTH * n_phys, PAGE_SIZE * SB_HEADS, HEAD_DIM)
    zeros = lambda *shape: jnp.zeros(shape, F32)
    outs = {name: [] for name in ("sp", "ss", "cp", "cs", "pp", "ps", "rp", "rs")}
    kv_p = kv_s = None

    w_in = jnp.swapaxes(w_in, 1, 2)
    wa, wdt, wb, w1 = convert_weights([("w_in", w_in, 0), ("plain", w_ff1, 0)])
    padded = lambda a: jnp.pad(a[:, None, :], ((0, 0), (0, CHUNK - 1), (0, 0))).reshape(bs * CHUNK, a.shape[1])
    first = lambda y: y.reshape(bs, CHUNK, -1)[:, 0, :]

    for l in range(DEPTH):
        ssd_w = (conv_w[l], conv_b[l], dt_bias[l], a_log[l], d_skip[l], ssd_norm_w[l])
        (pa, dt_raw, pb), (pa_s, dt_s, pb_s), wo = _dense_in(l, xp, xs, norm1_w[l], wa, wdt, wb, w_out)

        mix_p, cp, sp = ssd_mixer(pa, dt_raw, zeros(bp, 8, SSD_CONV_CH), zeros(bp, SSD_HEADS, 64, SSD_STATE), *ssd_w,
                                  nb=bp, nc=ncp, n_valid=CHUNK, mix="new")
        mix_p, pp = pool_mixer(pb, zeros(bp, POOL_BUF + 1, W_POOL), pool_w[l], pool_scale[l],
                               nb=bp, nt=seq // 256, tl=256, n_valid=256, start=0, mix=mix_p)
        mix_p, rp = retention_mixer(pb, zeros(bp, RET_HEADS, HEAD_DIM, HEAD_DIM), ret_gn_w[l],
                                    nb=bp, nc=ncp, n_valid=CHUNK, start=0, mix=mix_p)
        qn, kb, vb, kv_p = sb_prep(pb, sb_q_norm[l], sb_k_norm[l], l, kv_p)
        mix_p = sb_prompt_attention(qn, kb, vb, nb=bp, seq=seq, mix=mix_p)
        outs["sp"].append(sp); outs["cp"].append(cp); outs["pp"].append(pp); outs["rp"].append(rp)

        pa_sp, dt_sp, pb_sp = padded(pa_s), padded(dt_s), padded(pb_s)
        y_ssd, cs, ss = ssd_mixer(pa_sp, dt_sp, jnp.pad(state_conv[l], ((0, 0), (5, 0), (0, 0))), state_ssm[l], *ssd_w,
                                  nb=bs, nc=1, n_valid=1)
        y_pool, ps = pool_mixer(pb_sp, jnp.pad(state_pool[l], ((0, 0), (1, 0), (0, 0))), pool_w[l], pool_scale[l],
                                nb=bs, nt=1, tl=CHUNK, n_valid=1, start=past_len)
        y_ret, rs = retention_mixer(pb_sp, state_ret[l], ret_gn_w[l], nb=bs, nc=1, n_valid=1, start=past_len)
        qn, kb, vb, kv_s = sb_prep(pb_s, sb_q_norm[l], sb_k_norm[l], l, kv_s)
        y_sb = sb_decode_attention(qn, ck, cv, page_table + l * n_phys)
        mix_s = jnp.concatenate([first(y_ssd), first(y_pool), first(y_ret), y_sb], axis=1)
        outs["ss"].append(ss); outs["cs"].append(cs); outs["ps"].append(ps); outs["rs"].append(rs)

        xp, xs, w1_next, w_in_next = _dense_out(l, xp, xs, mix_p, mix_s, wo, norm2_w[l], w1, w_in, w_ff1, w_ff2)
        if l + 1 < DEPTH:
            (w1,), (wa, wdt, wb) = w1_next, w_in_next

    st = lambda name: jnp.stack(outs[name])
    kv = lambda a, nb, n: a.reshape(DEPTH, nb, n, SB_HEADS, HEAD_DIM)
    return (xp.reshape(bp, seq, D_MODEL), xs.reshape(bs, 1, D_MODEL),
            kv(kv_p[0], bp, seq), kv(kv_p[1], bp, seq), kv(kv_s[0], bs, 1), kv(kv_s[1], bs, 1),
            st("sp"), st("ss"), st("cp"), st("cs"), st("pp"), st("ps"), st("rp"), st("rs"))
```

```python
import functools
import math

import jax
import jax.numpy as jnp
from jax import lax
from jax.experimental import pallas as pl
from jax.experimental.pallas import tpu as pltpu

F32 = jnp.float32
BF16 = jnp.bfloat16

D_MODEL = 4096
DEPTH = 4
PAGE_SIZE = 128
W_SSD = W_POOL = W_RET = W_SB = 1024
SSD_HEADS = 16
SSD_GROUPS = 4
SSD_STATE = 128
SSD_GN = SSD_GROUPS * SSD_STATE
SSD_CONV = 4
SSD_CONV_CH = W_SSD + 2 * SSD_GN
POOL_WINDOWS = (2, 4, 8, 16)
POOL_GC = W_POOL // len(POOL_WINDOWS)
POOL_BUF = max(POOL_WINDOWS) - 1
RET_HEADS = 8
HEAD_DIM = 128
SB_HEADS = 8
ROPE_BASE = 10000.0
D_FF = 4 * D_MODEL
NORM_EPS = 1e-6
CHUNK = 128
DECODE_PAGES_PER_STEP = 8
PROJ_A = W_SSD * 2 + SSD_GN * 2
PROJ_B_COLS = ("u", "rq", "rk", "rv", "rg", "sq", "sk", "sv")
PROJ_B = 1024 * len(PROJ_B_COLS)
BF16_SUBLANES = 16
LANES = 128
NEG_BIG = -1e30
VMEM_LIMIT = 58 * 1024 * 1024

_NT = (((1,), (1,)), ((), ()))


def _params(sem):
    return pltpu.CompilerParams(dimension_semantics=sem, vmem_limit_bytes=VMEM_LIMIT)


def _silu(x):
    return x / (1.0 + jnp.exp(-x))


def _softplus(x):
    return jnp.maximum(x, 0.0) + jnp.log1p(jnp.exp(-jnp.abs(x)))


def _dot(a, b):
    return jnp.dot(a.astype(BF16), b.astype(BF16), preferred_element_type=F32)


def _dot_nt(a, b):
    return lax.dot_general(a.astype(BF16), b.astype(BF16), _NT, preferred_element_type=F32)


def _rmsnorm_kernel(x_ref, w_ref, o_ref):
    x = x_ref[...]
    ms = jnp.mean(x * x, axis=-1, keepdims=True)
    o_ref[...] = (x * lax.rsqrt(ms + NORM_EPS) * w_ref[...]).astype(o_ref.dtype)


def rmsnorm_bf16(x, w):
    m, d = x.shape
    tr = min(m, 512)
    return pl.pallas_call(
        _rmsnorm_kernel,
        grid=(m // tr,),
        in_specs=[pl.BlockSpec((tr, d), lambda i: (i, 0)), pl.BlockSpec((1, d), lambda i: (0, 0))],
        out_specs=pl.BlockSpec((tr, d), lambda i: (i, 0)),
        out_shape=jax.ShapeDtypeStruct((m, d), BF16),
        compiler_params=_params(("parallel",)),
        name="rmsnorm",
    )(x, w.reshape(1, d))


def _mm_kernel(*refs, nk, epilogue, casts, w_transposed, w_in_period):
    xb_ref, xs_ref, w_ref = refs[:3]
    n_res = 2 if epilogue == "residual" else 0
    rb_ref, rs_ref = refs[3:5] if n_res else (None, None)
    n_src = sum(_N_SRC[c] for c in casts)
    n_in = 3 + n_res + n_src
    src_refs = refs[3 + n_res:n_in]
    ob_ref, os_ref = refs[n_in:n_in + 2]
    dst_refs = refs[n_in + 2:]
    k = pl.program_id(2)
    flat = (pl.program_id(0) * pl.num_programs(1) + pl.program_id(1)) * nk + k
    _convert_blocks(casts, src_refs, dst_refs, w_in_fresh=flat % w_in_period == 0)

    def run(x_ref, r_ref, o_ref):
        if w_transposed:
            part = lax.dot_general(x_ref[...], w_ref[...], _NT, preferred_element_type=F32)
        else:
            part = jnp.dot(x_ref[...], w_ref[...], preferred_element_type=F32)
        if epilogue == "relu2":
            r = jnp.maximum(part, 0.0)
            o_ref[...] = (r * r).astype(o_ref.dtype)
        elif nk == 1:
            o_ref[...] = part if r_ref is None else part + r_ref[...]
        else:
            @pl.when(k == 0)
            def _():
                o_ref[...] = part if r_ref is None else part + r_ref[...]

            @pl.when(k > 0)
            def _():
                o_ref[...] += part

    run(xb_ref, rb_ref, ob_ref)

    @pl.when(pl.program_id(1) == 0)
    def _():
        run(xs_ref, rs_ref, os_ref)


W_IN_STEPS = 64
_A_ROWS = PROJ_A // W_IN_STEPS
_B_ROWS = PROJ_B // W_IN_STEPS
_DT_BLOCKS = LANES // BF16_SUBLANES
_N_SRC = {"plain": 1, "w_in": 4}
_N_DST = {"plain": 1, "w_in": 3}


def _convert_blocks(kinds, src_refs, dst_refs, w_in_fresh=True):
    src_refs, dst_refs = list(src_refs), list(dst_refs)
    for kind in kinds:
        if kind == "w_in":
            src_a, src_dt, src_b, src_tail = (src_refs.pop(0) for _ in range(4))
            dst_a, dst_dt, dst_b = (dst_refs.pop(0) for _ in range(3))

            def split(src_a=src_a, src_dt=src_dt, src_b=src_b, src_tail=src_tail, dst_a=dst_a, dst_dt=dst_dt, dst_b=dst_b):
                dst_a[...] = src_a[...].astype(BF16)
                dst_dt[...] = src_dt[...].astype(BF16)
                dst_b[:_B_ROWS - SSD_HEADS, :] = src_b[SSD_HEADS:, :].astype(BF16)
                dst_b[_B_ROWS - SSD_HEADS:, :] = src_tail[...].astype(BF16)

            split() if w_in_fresh is True else pl.when(w_in_fresh)(split)
        else:
            dst_refs.pop(0)[...] = src_refs.pop(0)[...].astype(BF16)


def _src_spec(n, cols, layer, block):
    return pl.BlockSpec((None, n, cols), lambda *g: (layer, block(*g), 0))


def _dst_spec(n, cols, block):
    return pl.BlockSpec((n, cols), lambda *g: (block(*g), 0))


def _convert_specs(casts, total, flat):
    in_specs, args, out_specs, out_shape = [], [], [], []
    for kind, src, layer in casts:
        rows, cols = src.shape[1:]
        shape = lambda n: jax.ShapeDtypeStruct((n, cols), BF16)
        if kind == "w_in":
            assert rows == PROJ_A + SSD_HEADS + PROJ_B and SSD_HEADS == BF16_SUBLANES and total % W_IN_STEPS == 0
            coarse = lambda *g: flat(*g) // (total // W_IN_STEPS)
            dt_block = lambda *g: jnp.minimum(coarse(*g), _DT_BLOCKS - 1)
            in_specs += [
                _src_spec(_A_ROWS, cols, layer, coarse),
                _src_spec(BF16_SUBLANES, cols, layer, lambda *g: PROJ_A // BF16_SUBLANES + dt_block(*g)),
                _src_spec(_B_ROWS, cols, layer, lambda *g: PROJ_A // _B_ROWS + coarse(*g)),
                _src_spec(SSD_HEADS, cols, layer,
                          lambda *g: (PROJ_A + _B_ROWS) // SSD_HEADS + coarse(*g) * (_B_ROWS // SSD_HEADS)),
            ]
            args += [src] * 4
            out_specs += [_dst_spec(_A_ROWS, cols, coarse), _dst_spec(BF16_SUBLANES, cols, dt_block),
                          _dst_spec(_B_ROWS, cols, coarse)]
            out_shape += [shape(PROJ_A), shape(LANES), shape(PROJ_B)]
        else:
            rs = rows // total
            assert rs * total == rows and rs % BF16_SUBLANES == 0
            in_specs.append(_src_spec(rs, cols, layer, flat))
            args.append(src)
            out_specs.append(_dst_spec(rs, cols, flat))
            out_shape.append(shape(rows))
    return in_specs, args, out_specs, out_shape


def convert_weights(casts):
    kinds = tuple(c[0] for c in casts)
    in_specs, args, out_specs, out_shape = _convert_specs(casts, W_IN_STEPS, lambda i: i)
    n_src = sum(_N_SRC[c] for c in kinds)

    def body(*refs):
        _convert_blocks(kinds, refs[:n_src], refs[n_src:])

    return pl.pallas_call(body, grid=(W_IN_STEPS,), in_specs=in_specs, out_specs=out_specs, out_shape=out_shape,
                          compiler_params=_params(("arbitrary",)), name="convert_weights")(*args)


def matmul(xb, xs, w, *, tm, tn, tk, out_dtype=F32, epilogue=None, residual=None, casts=(), w_transposed=False,
           name="matmul"):
    m, kdim = xb.shape
    ms = xs.shape[0]
    n = w.shape[0] if w_transposed else w.shape[1]
    tm, tn, tk = min(tm, m), min(tn, n), min(tk, kdim)
    nk = kdim // tk
    ni = m // tm
    n_steps = (n // tn) * ni * nk
    step = lambda j, i, k: (j * ni + i) * nk + k
    w_spec = pl.BlockSpec((tn, tk), lambda j, i, k: (j, k)) if w_transposed else pl.BlockSpec((tk, tn), lambda j, i, k: (k, j))
    in_specs = [
        pl.BlockSpec((tm, tk), lambda j, i, k: (i, k)),
        pl.BlockSpec((ms, tk), lambda j, i, k: (0, k)),
        w_spec,
    ]
    args = [xb, xs, w]
    out_specs = [pl.BlockSpec((tm, tn), lambda j, i, k: (i, j)), pl.BlockSpec((ms, tn), lambda j, i, k: (0, j))]
    out_shape = [jax.ShapeDtypeStruct((m, n), out_dtype), jax.ShapeDtypeStruct((ms, n), out_dtype)]
    if epilogue == "residual":
        in_specs += [pl.BlockSpec((tm, tn), lambda j, i, k: (i, j)), pl.BlockSpec((ms, tn), lambda j, i, k: (0, j))]
        args += list(residual)
    assert nk == 1 or out_dtype == F32
    c_in, c_args, c_out, c_shape = _convert_specs(casts, n_steps, step)
    return pl.pallas_call(
        functools.partial(_mm_kernel, nk=nk, epilogue=epilogue, casts=tuple(c[0] for c in casts),
                          w_transposed=w_transposed, w_in_period=max(n_steps // W_IN_STEPS, 1)),
        grid=(n // tn, ni, nk),
        in_specs=in_specs + c_in,
        out_specs=out_specs + c_out,
        out_shape=out_shape + c_shape,
        compiler_params=_params(("arbitrary", "arbitrary", "arbitrary")),
        name=name,
    )(*args, *c_args)


def _ssd_kernel(xs_ref, z_ref, bm_ref, cm_ref, dt_ref, cs_ref, s0_ref, cw_ref, cb_ref, dtb_ref, alog_ref,
                dsk_ref, nw_ref, y_ref, cnew_ref, snew_ref, ext_ref, s_ref, *, nc, n_valid):
    c = pl.program_id(1)
    C = CHUNK

    @pl.when(c == 0)
    def _():
        ext_ref[0:8, :] = cs_ref[0]
        s_ref[...] = s0_ref[0]

    ext_ref[8:8 + C, 0:W_SSD] = xs_ref[...]
    ext_ref[8:8 + C, W_SSD:W_SSD + SSD_GN] = bm_ref[...]
    ext_ref[8:8 + C, W_SSD + SSD_GN:] = cm_ref[...]
    cw = cw_ref[...]
    conv = cb_ref[...] + ext_ref[8:8 + C, :] * cw[3:4, :]
    for tap in range(SSD_CONV - 1):
        conv = conv + ext_ref[5 + tap:5 + tap + C, :] * cw[tap:tap + 1, :]

    @pl.when(c == nc - 1)
    def _():
        cnew_ref[0] = ext_ref[5 + n_valid:8 + n_valid, :]

    ext_ref[0:8, :] = ext_ref[C:C + 8, :]

    xbc = _silu(conv)
    x = xbc[:, :W_SSD]
    bmat = xbc[:, W_SSD:W_SSD + SSD_GN]
    cmat = xbc[:, W_SSD + SSD_GN:]

    dt = _softplus(dt_ref[...] + dtb_ref[...])
    if n_valid < C:
        dt = jnp.where(lax.broadcasted_iota(jnp.int32, (C, LANES), 0) < n_valid, dt, 0.0)
    da = dt * (-jnp.exp(alog_ref[...]))
    ri = lax.broadcasted_iota(jnp.int32, (C, C), 0)
    ci = lax.broadcasted_iota(jnp.int32, (C, C), 1)
    tril = ri >= ci
    cum = jnp.dot(tril.astype(F32), da, precision=lax.Precision.HIGHEST, preferred_element_type=F32)
    cum_t = cum.T
    end = cum[C - 1:C, :]
    e_end = jnp.exp(end)
    lo = lax.broadcasted_iota(jnp.int32, (C, LANES), 1) < 64
    row_lo = lax.broadcasted_iota(jnp.int32, (C, LANES), 0) < 64
    dsk = dsk_ref[...]

    def pick(v, h0):
        return jnp.where(lo, v[:, h0:h0 + 1], v[:, h0 + 1:h0 + 2])

    ys = []
    for g in range(SSD_GROUPS):
        bg = bmat[:, g * SSD_STATE:(g + 1) * SSD_STATE].astype(BF16)
        cg = cmat[:, g * SSD_STATE:(g + 1) * SSD_STATE].astype(BF16)
        cb = _dot_nt(cg, bg)
        for pr in range(2):
            p = 2 * g + pr
            h0 = 2 * p
            xp = x[:, p * LANES:(p + 1) * LANES]
            xdt = xp * pick(dt, h0)
            y_in = jnp.zeros((C, LANES), F32)
            for h, keep in ((h0, lo), (h0 + 1, jnp.logical_not(lo))):
                seg = cum[:, h:h + 1] - cum_t[h:h + 1, :]
                decay = jnp.exp(jnp.where(tril, seg, NEG_BIG))
                y_in = y_in + _dot(cb * decay, jnp.where(keep, xdt, 0.0))
            sp = s_ref[p]
            y_off = _dot_nt(cg, sp) * jnp.exp(pick(cum, h0))
            ys.append(y_in + y_off + jnp.where(lo[0:1], dsk[:, h0:h0 + 1], dsk[:, h0 + 1:h0 + 2]) * xp)
            w_end = jnp.exp(pick(jnp.broadcast_to(end, (C, LANES)) - cum, h0))
            s_scale = jnp.where(row_lo, e_end[:, h0:h0 + 1], e_end[:, h0 + 1:h0 + 2])
            s_ref[p] = sp * s_scale + _dot((xdt * w_end).T, bg)

    y = jnp.concatenate(ys, axis=1) * _silu(z_ref[...])
    ms = jnp.mean(y * y, axis=-1, keepdims=True)
    y_ref[:, :W_SSD] = (y * lax.rsqrt(ms + NORM_EPS) * nw_ref[...]).astype(y_ref.dtype)
    if y_ref.shape[1] > W_SSD:
        y_ref[:, W_SSD:] = jnp.zeros((C, y_ref.shape[1] - W_SSD), y_ref.dtype)

    @pl.when(c == nc - 1)
    def _():
        snew_ref[0] = s_ref[...]


def _mix_target(rows, mix, col, kernel_fn, n_in):
    if mix is None:
        return kernel_fn, jax.ShapeDtypeStruct((rows, 1024), BF16), 0, {}, [], []
    shape = jax.ShapeDtypeStruct((rows, D_MODEL), BF16)
    if isinstance(mix, str):
        return kernel_fn, shape, col, {}, [], []
    skipping = lambda *refs: kernel_fn(*refs[:n_in], *refs[n_in + 1:])
    return skipping, shape, col, {"input_output_aliases": {n_in: 0}}, [pl.BlockSpec(memory_space=pl.ANY)], [mix]


def ssd_mixer(proj, dt_raw, conv_state8, ssm_state, conv_w, conv_b, dt_bias, a_log, d_skip, norm_w, *, nb, nc, n_valid,
              mix=None):
    assert n_valid == CHUNK or nc == 1
    rows = nb * nc * CHUNK
    pad = lambda v: jnp.pad(v, (0, LANES - SSD_HEADS)).reshape(1, LANES)
    row = lambda b, c: b * nc + c
    full = lambda shape: pl.BlockSpec(shape, lambda b, c: (0,) * len(shape))
    n_pair = SSD_HEADS // 2
    kern, y_shape, ycol, alias, x_specs, x_args = _mix_target(
        rows, mix, 0, functools.partial(_ssd_kernel, nc=nc, n_valid=n_valid), 13)
    y, conv_new, ssm_new = pl.pallas_call(
        kern,
        grid=(nb, nc),
        **alias,
        in_specs=[
            pl.BlockSpec((CHUNK, W_SSD), lambda b, c: (row(b, c), 0)),
            pl.BlockSpec((CHUNK, W_SSD), lambda b, c: (row(b, c), 1)),
            pl.BlockSpec((CHUNK, SSD_GN), lambda b, c: (row(b, c), 4)),
            pl.BlockSpec((CHUNK, SSD_GN), lambda b, c: (row(b, c), 5)),
            pl.BlockSpec((CHUNK, LANES), lambda b, c: (row(b, c), 0)),
            pl.BlockSpec((1, 8, SSD_CONV_CH), lambda b, c: (b, 0, 0)),
            pl.BlockSpec((1, n_pair, LANES, SSD_STATE), lambda b, c: (b, 0, 0, 0)),
            full((SSD_CONV, SSD_CONV_CH)),
            full((1, SSD_CONV_CH)),
            full((1, LANES)),
            full((1, LANES)),
            full((1, LANES)),
            full((1, W_SSD)),
        ] + x_specs,
        out_specs=[
            pl.BlockSpec((CHUNK, y_shape.shape[1]), lambda b, c: (row(b, c), 0)),
            pl.BlockSpec((1, SSD_CONV - 1, SSD_CONV_CH), lambda b, c: (b, 0, 0)),
            pl.BlockSpec((1, n_pair, LANES, SSD_STATE), lambda b, c: (b, 0, 0, 0)),
        ],
        out_shape=[
            y_shape,
            jax.ShapeDtypeStruct((nb, SSD_CONV - 1, SSD_CONV_CH), F32),
            jax.ShapeDtypeStruct((nb, n_pair, LANES, SSD_STATE), F32),
        ],
        scratch_shapes=[pltpu.VMEM((CHUNK + 8, SSD_CONV_CH), F32), pltpu.VMEM((n_pair, LANES, SSD_STATE), F32)],
        compiler_params=_params(("parallel", "arbitrary")),
        name="ssd_mixer",
    )(proj, proj, proj, proj, dt_raw, conv_state8, ssm_state.reshape(nb, n_pair, LANES, SSD_STATE),
      conv_w, conv_b.reshape(1, -1), pad(dt_bias), pad(a_log), pad(d_skip), norm_w.reshape(1, -1), *x_args)
    return y, conv_new, ssm_new.reshape(nb, SSD_HEADS, 64, SSD_STATE)


def _pool_kernel(u_ref, buf_ref, pw_ref, ps_ref, y_ref, bnew_ref, ext_ref, *, nt, tl, n_valid, start):
    t = pl.program_id(1)
    H = POOL_BUF + 1

    @pl.when(t == 0)
    def _():
        ext_ref[0:H, :] = buf_ref[0]

    ext_ref[H:H + tl, :] = u_ref[...]
    pos = start + t * tl + lax.broadcasted_iota(jnp.int32, (tl, 1), 0)
    outs = []
    for g, w in enumerate(POOL_WINDOWS):
        cols = slice(g * POOL_GC, (g + 1) * POOL_GC)
        u = ext_ref[H:H + tl, cols]
        ssum = u
        for k in range(1, w):
            ssum = ssum + ext_ref[H - k:H - k + tl, cols]
        cnt = jnp.minimum(pos + 1, w).astype(F32)
        d = ssum / cnt - u
        outs.append(_dot(d, pw_ref[g]))
    y_ref[...] = (jnp.concatenate(outs, axis=1) * ps_ref[...]).astype(y_ref.dtype)

    @pl.when(t == nt - 1)
    def _():
        bnew_ref[0] = ext_ref[1 + n_valid:H + n_valid, :]

    ext_ref[0:H, :] = ext_ref[tl:tl + H, :]


def pool_mixer(proj, buf16, pool_w, pool_scale, *, nb, nt, tl, n_valid, start, mix=None):
    assert n_valid == tl or nt == 1
    rows = nb * nt * tl
    ublk = PROJ_B_COLS.index("u")
    kern, y_shape, ycol, alias, x_specs, x_args = _mix_target(
        rows, mix, 1, functools.partial(_pool_kernel, nt=nt, tl=tl, n_valid=n_valid, start=start), 4)
    return pl.pallas_call(
        kern,
        grid=(nb, nt),
        **alias,
        in_specs=[
            pl.BlockSpec((tl, W_POOL), lambda b, t: (b * nt + t, ublk)),
            pl.BlockSpec((1, POOL_BUF + 1, W_POOL), lambda b, t: (b, 0, 0)),
            pl.BlockSpec((len(POOL_WINDOWS), POOL_GC, POOL_GC), lambda b, t: (0, 0, 0)),
            pl.BlockSpec((1, W_POOL), lambda b, t: (0, 0)),
        ] + x_specs,
        out_specs=[
            pl.BlockSpec((tl, W_POOL), lambda b, t: (b * nt + t, ycol)),
            pl.BlockSpec((1, POOL_BUF, W_POOL), lambda b, t: (b, 0, 0)),
        ],
        out_shape=[y_shape, jax.ShapeDtypeStruct((nb, POOL_BUF, W_POOL), F32)],
        scratch_shapes=[pltpu.VMEM((tl + POOL_BUF + 1, W_POOL), F32)],
        compiler_params=_params(("parallel", "arbitrary")),
        name="pool_mixer",
    )(proj, buf16, pool_w.astype(BF16), pool_scale.reshape(1, -1), *x_args)


def _ret_kernel(q_ref, k_ref, v_ref, g_ref, cos_ref, sin_ref, s0_ref, gnw_ref, y_ref, snew_ref, s_ref, *, nc, n_valid):
    c = pl.program_id(1)
    C = CHUNK

    @pl.when(c == 0)
    def _():
        s_ref[...] = s0_ref[0]

    cos_f = cos_ref[...]
    sin_f = sin_ref[...]
    ri = lax.broadcasted_iota(jnp.int32, (C, C), 0)
    ci = lax.broadcasted_iota(jnp.int32, (C, C), 1)
    causal = ri >= ci
    diff = jnp.where(causal, ri - ci, 0).astype(F32)
    rowi = lax.broadcasted_iota(jnp.int32, (C, 1), 0)
    rowf = rowi.astype(F32)
    for h in range(RET_HEADS):
        lg = math.log1p(-(2.0 ** (-5 - h)))
        sl = slice(h * HEAD_DIM, (h + 1) * HEAD_DIM)
        qh, kh, vh = q_ref[:, sl], k_ref[:, sl], v_ref[:, sl]
        qr = (qh * cos_f + pltpu.roll(qh, HEAD_DIM // 2, 1) * sin_f) * (HEAD_DIM ** -0.5)
        kr = kh * cos_f + pltpu.roll(kh, HEAD_DIM // 2, 1) * sin_f
        intra = jnp.where(causal, jnp.exp(lg * diff), 0.0)
        s = _dot_nt(qr, kr) * intra
        q_dec = jnp.exp(lg * (rowf + 1.0))
        k_dec = jnp.where(rowi < n_valid, jnp.exp(lg * (n_valid - 1.0 - rowf)), 0.0)
        sh = s_ref[h]
        y = _dot(s, vh) + _dot(qr * q_dec, sh)
        s_ref[h] = sh * math.exp(lg * n_valid) + _dot((kr * k_dec).T, vh)
        mu = jnp.mean(y, axis=-1, keepdims=True)
        yc = y - mu
        var = jnp.mean(yc * yc, axis=-1, keepdims=True)
        yn = yc * lax.rsqrt(var + NORM_EPS) * gnw_ref[:, sl]
        y_ref[:, sl] = (_silu(g_ref[:, sl]) * yn).astype(y_ref.dtype)

    @pl.when(c == nc - 1)
    def _():
        snew_ref[0] = s_ref[...]


def _rope_tables(start, n):
    half = HEAD_DIM // 2
    inv = ROPE_BASE ** (-jnp.arange(half, dtype=F32) / half)
    ang = (start + jnp.arange(n)).astype(F32)[:, None] * inv[None, :]
    cos, sin = jnp.cos(ang), jnp.sin(ang)
    return jnp.concatenate([cos, cos], axis=1), jnp.concatenate([-sin, sin], axis=1)


def retention_mixer(proj, ret_state, gn_w, *, nb, nc, n_valid, start, mix=None):
    assert n_valid == CHUNK or nc == 1
    rows = nb * nc * CHUNK
    cos_f, sin_f = _rope_tables(start, nc * CHUNK)
    col = lambda j: pl.BlockSpec((CHUNK, W_RET), lambda b, c: (b * nc + c, j))
    kern, y_shape, ycol, alias, x_specs, x_args = _mix_target(
        rows, mix, 2, functools.partial(_ret_kernel, nc=nc, n_valid=n_valid), 8)
    return pl.pallas_call(
        kern,
        grid=(nb, nc),
        **alias,
        in_specs=[
            *[col(PROJ_B_COLS.index(name)) for name in ("rq", "rk", "rv", "rg")],
            pl.BlockSpec((CHUNK, HEAD_DIM), lambda b, c: (c, 0)),
            pl.BlockSpec((CHUNK, HEAD_DIM), lambda b, c: (c, 0)),
            pl.BlockSpec((1, RET_HEADS, HEAD_DIM, HEAD_DIM), lambda b, c: (b, 0, 0, 0)),
            pl.BlockSpec((1, W_RET), lambda b, c: (0, 0)),
        ] + x_specs,
        out_specs=[
            pl.BlockSpec((CHUNK, W_RET), lambda b, c: (b * nc + c, ycol)),
            pl.BlockSpec((1, RET_HEADS, HEAD_DIM, HEAD_DIM), lambda b, c: (b, 0, 0, 0)),
        ],
        out_shape=[y_shape, jax.ShapeDtypeStruct((nb, RET_HEADS, HEAD_DIM, HEAD_DIM), F32)],
        scratch_shapes=[pltpu.VMEM((RET_HEADS, HEAD_DIM, HEAD_DIM), F32)],
        compiler_params=_params(("parallel", "arbitrary")),
        name="retention_mixer",
    )(proj, proj, proj, proj, cos_f, sin_f, ret_state, gn_w.reshape(1, -1), *x_args)


def _sb_prep_kernel(q_ref, k_ref, v_ref, qw_ref, kw_ref, qn_ref, kn_ref, kb_ref, vf_ref, vb_ref, *, layer, first):
    if first:
        for other in range(DEPTH):
            if other != layer:
                kn_ref[other] = jnp.zeros(kn_ref.shape[1:], F32)
                vf_ref[other] = jnp.zeros(vf_ref.shape[1:], F32)
        kn_ref, vf_ref = kn_ref.at[layer], vf_ref.at[layer]
    for h in range(SB_HEADS):
        sl = slice(h * HEAD_DIM, (h + 1) * HEAD_DIM)
        q = q_ref[:, sl]
        k = k_ref[:, sl]
        qn = q * lax.rsqrt(jnp.mean(q * q, axis=-1, keepdims=True) + NORM_EPS) * qw_ref[...]
        kn = k * lax.rsqrt(jnp.mean(k * k, axis=-1, keepdims=True) + NORM_EPS) * kw_ref[...]
        qn_ref[:, sl] = qn.astype(BF16)
        kn_ref[:, sl] = kn
        kb_ref[:, sl] = kn.astype(BF16)
    v = v_ref[...]
    vf_ref[...] = v
    vb_ref[...] = v.astype(BF16)


def sb_prep(proj, q_norm, k_norm, layer, kv_stacks=None):
    m = proj.shape[0]
    tr = min(m, 256)
    col = lambda j: pl.BlockSpec((tr, W_SB), lambda i: (i, j))
    out = pl.BlockSpec((tr, W_SB), lambda i: (i, 0))
    first = kv_stacks is None
    slot = (pl.BlockSpec((DEPTH, tr, W_SB), lambda i: (0, i, 0)) if first
            else pl.BlockSpec((None, tr, W_SB), lambda i: (layer, i, 0)))
    sds = lambda dt: jax.ShapeDtypeStruct((m, W_SB), dt)
    stack = jax.ShapeDtypeStruct((DEPTH, m, W_SB), F32)
    n_in = 5
    body = functools.partial(_sb_prep_kernel, layer=layer, first=first)
    kern, alias, x_specs, x_args = body, {}, [], []
    if not first:
        kern = lambda *refs: body(*refs[:n_in], *refs[n_in + 2:])
        alias = {"input_output_aliases": {n_in: 1, n_in + 1: 3}}
        x_specs, x_args = [pl.BlockSpec(memory_space=pl.ANY)] * 2, list(kv_stacks)
    qn, k_stack, kb, v_stack, vb = pl.pallas_call(
        kern,
        grid=(m // tr,),
        **alias,
        in_specs=[*[col(PROJ_B_COLS.index(name)) for name in ("sq", "sk", "sv")],
                  pl.BlockSpec((1, HEAD_DIM), lambda i: (0, 0)),
                  pl.BlockSpec((1, HEAD_DIM), lambda i: (0, 0))] + x_specs,
        out_specs=[out, slot, out, slot, out],
        out_shape=[sds(BF16), stack, sds(BF16), stack, sds(BF16)],
        compiler_params=_params(("parallel",)),
        name="sb_prep",
    )(proj, proj, proj, q_norm.reshape(1, -1), k_norm.reshape(1, -1), *x_args)
    return qn, kb, vb, (k_stack, v_stack)


def _log_sigmoid_pair(zz):
    lb = jnp.minimum(zz, 0.0) - jnp.log(1.0 + jnp.exp(-jnp.abs(zz)))
    return lb, lb - zz


def _suffix_matrix(t):
    row = lax.broadcasted_iota(jnp.int32, (t, 2 * t), 0)
    col = lax.broadcasted_iota(jnp.int32, (t, 2 * t), 1)
    return jnp.logical_or(col >= t, row > col).astype(BF16)


def _suffix_and_total(l1, suffix_matrix):
    t = l1.shape[1]
    hi = l1.astype(BF16)
    lo = (l1 - hi.astype(F32)).astype(BF16)
    both = (jnp.dot(hi, suffix_matrix, preferred_element_type=F32)
            + jnp.dot(lo, suffix_matrix, preferred_element_type=F32))
    return both[:, :t], both[:, t:]


def _sb_prompt_kernel(q_ref, k_ref, v_ref, o_ref, acc_ref, run_ref):
    qi = pl.program_id(1)
    T = CHUNK
    ri = lax.broadcasted_iota(jnp.int32, (T, T), 0)
    ci = lax.broadcasted_iota(jnp.int32, (T, T), 1)
    allowed = ci < ri
    sfx = _suffix_matrix(T)
    scale = HEAD_DIM ** -0.5

    def tiles_step(kblks, diagonal):
        offs = [pl.multiple_of(kb * T, T) for kb in kblks]
        tiles, heads = range(len(kblks)), range(SB_HEADS)
        sl = lambda h: slice(h * HEAD_DIM, (h + 1) * HEAD_DIM)
        zz = [[lax.dot_general(q_ref[:, sl(h)], k_ref[pl.ds(offs[t], T), sl(h)], _NT, preferred_element_type=F32)
               for h in heads] for t in tiles]
        logs = [[_log_sigmoid_pair(zz[t][h] * scale) for h in heads] for t in tiles]
        l1 = [[jnp.where(allowed, logs[t][h][1], 0.0) if diagonal else logs[t][h][1] for h in heads] for t in tiles]
        sums = [[_suffix_and_total(l1[t][h], sfx) for h in heads] for t in tiles]
        weights = [[None] * SB_HEADS for _ in tiles]
        for h in heads:
            run = None if diagonal else run_ref[h]
            for t in tiles:
                log_a = logs[t][h][0] + sums[t][h][0]
                a = jnp.exp(log_a if run is None else log_a + run)
                weights[t][h] = (jnp.where(allowed, a, 0.0) if diagonal else a).astype(BF16)
                run = sums[t][h][1] if run is None else run + sums[t][h][1]
            run_ref[h] = run
        parts = [[jnp.dot(weights[t][h], v_ref[pl.ds(offs[t], T), sl(h)], preferred_element_type=F32)
                  for h in heads] for t in tiles]
        for h in heads:
            tot = parts[0][h]
            for t in tiles[1:]:
                tot = tot + parts[t][h]
            acc_ref[h] = tot if diagonal else acc_ref[h] + tot

    tiles_step([qi], True)

    def body(t, carry):
        tiles_step([qi - 1 - 4 * t - u for u in range(4)], False)
        return carry

    lax.fori_loop(0, qi // 4, body, 0)
    rest = qi % 4

    @pl.when(rest >= 2)
    def _():
        tiles_step([rest - 1, rest - 2], False)

    @pl.when(rest % 2 == 1)
    def _():
        tiles_step([0], False)
    for h in range(SB_HEADS):
        o_ref[:, h * HEAD_DIM:(h + 1) * HEAD_DIM] = acc_ref[h].astype(o_ref.dtype)


def sb_prompt_attention(qn, kb, vb, *, nb, seq, mix=None):
    nq = seq // CHUNK
    kern, y_shape, ycol, alias, x_specs, x_args = _mix_target(nb * seq, mix, 3, _sb_prompt_kernel, 3)
    return pl.pallas_call(
        kern,
        grid=(nb, nq),
        **alias,
        in_specs=[
            pl.BlockSpec((CHUNK, W_SB), lambda b, i: (b * nq + i, 0)),
            pl.BlockSpec((seq, W_SB), lambda b, i: (b, 0)),
            pl.BlockSpec((seq, W_SB), lambda b, i: (b, 0)),
        ] + x_specs,
        out_specs=pl.BlockSpec((CHUNK, W_SB), lambda b, i: (b * nq + i, ycol)),
        out_shape=y_shape,
        scratch_shapes=[pltpu.VMEM((SB_HEADS, CHUNK, HEAD_DIM), F32), pltpu.VMEM((SB_HEADS, CHUNK, CHUNK), F32)],
        compiler_params=_params(("parallel", "arbitrary")),
        name="sb_prompt_attention",
    )(qn, kb, vb, *x_args)


def _sb_decode_kernel(pt_ref, q_ref, *refs, n_steps):
    G = DECODE_PAGES_PER_STEP
    k_refs, v_refs = refs[:G], refs[G:2 * G]
    o_ref, acc_ref, run_ref = refs[2 * G:]
    s = pl.program_id(1)

    @pl.when(s == 0)
    def _():
        acc_ref[...] = jnp.zeros_like(acc_ref)
        run_ref[...] = jnp.zeros_like(run_ref)

    P = PAGE_SIZE
    q = q_ref[0]
    head_row = lax.broadcasted_iota(jnp.int32, (SB_HEADS, P), 0)
    sfx = _suffix_matrix(P)
    head_rows = lambda ref, h: ref[pl.ds(h, P, stride=SB_HEADS), :].astype(BF16)

    scores = [[lax.dot_general(q, head_rows(k_refs[r], h), _NT, preferred_element_type=F32)
               for h in range(SB_HEADS)] for r in range(G)]
    logs = []
    for r in range(G):
        zz = scores[r][0]
        for h in range(1, SB_HEADS):
            zz = jnp.where(head_row == h, scores[r][h], zz)
        logs.append(_log_sigmoid_pair(zz * (HEAD_DIM ** -0.5)))
    sums = [_suffix_and_total(l1, sfx) for _, l1 in logs]
    run = run_ref[...]
    weights = []
    for r in range(G):
        weights.append(jnp.exp(logs[r][0] + sums[r][0] + run).astype(BF16))
        run = run + sums[r][1]
    run_ref[...] = run
    parts = [[jnp.dot(weights[r], head_rows(v_refs[r], h), preferred_element_type=F32)
              for h in range(SB_HEADS)] for r in range(G)]
    acc = acc_ref[...]
    for h in range(SB_HEADS):
        tot = parts[0][h]
        for r in range(1, G):
            tot = tot + parts[r][h]
        acc = acc + jnp.where(head_row == h, tot, 0.0)
    acc_ref[...] = acc

    @pl.when(s == n_steps - 1)
    def _():
        o_ref[0] = acc


def sb_decode_attention(qn, cache_k, cache_v, pages):
    nb, n_pages = pages.shape
    G = DECODE_PAGES_PER_STEP
    n_steps = n_pages // G
    page = lambda r: pl.BlockSpec((None, PAGE_SIZE * SB_HEADS, HEAD_DIM),
                                  lambda b, s, pt: (pt[b, n_pages - 1 - (s * G + r)], 0, 0))
    grid_spec = pltpu.PrefetchScalarGridSpec(
        num_scalar_prefetch=1,
        grid=(nb, n_steps),
        in_specs=[pl.BlockSpec((1, SB_HEADS, HEAD_DIM), lambda b, s, pt: (b, 0, 0))]
        + [page(r) for r in range(G)] * 2,
        out_specs=pl.BlockSpec((1, SB_HEADS, HEAD_DIM), lambda b, s, pt: (b, 0, 0)),
        scratch_shapes=[pltpu.VMEM((SB_HEADS, HEAD_DIM), F32), pltpu.VMEM((SB_HEADS, PAGE_SIZE), F32)],
    )
    out = pl.pallas_call(
        functools.partial(_sb_decode_kernel, n_steps=n_steps),
        grid_spec=grid_spec,
        out_shape=jax.ShapeDtypeStruct((nb, SB_HEADS, HEAD_DIM), F32),
        compiler_params=_params(("parallel", "arbitrary")),
        name="sb_decode_attention",
    )(pages, qn.reshape(nb, SB_HEADS, HEAD_DIM), *([cache_k] * G), *([cache_v] * G))
    return out.reshape(nb, W_SB).astype(BF16)


def _dense_in(l, xp, xs, norm_w, wa, wdt, wb, w_out):
    hp, hs = rmsnorm_bf16(xp, norm_w), rmsnorm_bf16(xs, norm_w)
    proj = functools.partial(matmul, hp, hs, tm=1024, tk=D_MODEL, w_transposed=True)
    pa, pa_s = proj(wa, tn=1024, name="in_proj_a")
    dt, dt_s = proj(wdt, tn=LANES, name="dt_proj")
    pb, pb_s, wo = proj(wb, tn=1024, casts=[("plain", w_out, l)], name="in_proj_b")
    return (pa, dt, pb), (pa_s, dt_s, pb_s), wo


def _dense_out(l, xp, xs, mix_p, mix_s, wo, norm2_w, w1, w_in, w_ff1, w_ff2):
    more = l + 1 < DEPTH
    xp, xs = matmul(mix_p, mix_s, wo, tm=1024, tn=1024, tk=D_MODEL, epilogue="residual", residual=(xp, xs),
                    name="out_proj")
    hp, hs = rmsnorm_bf16(xp, norm2_w), rmsnorm_bf16(xs, norm2_w)
    fp, fs, w2, *w1_next = matmul(hp, hs, w1, tm=1024, tn=1024, tk=D_MODEL, out_dtype=BF16, epilogue="relu2",
                                  casts=[("plain", w_ff2, l)] + ([("plain", w_ff1, l + 1)] if more else []),
                                  name="ffn_up")
    xp, xs, *w_in_next = matmul(fp, fs, w2, tm=1024, tn=1024, tk=2048, epilogue="residual", residual=(xp, xs),
                                casts=[("w_in", w_in, l + 1)] if more else [], name="ffn_down")
    return xp, xs, w1_next, w_in_next


def kernel(x_prompt, x_sample, cache_k, cache_v, state_ssm, state_conv, state_pool, state_ret, page_table, norm1_w, w_in, conv_w, conv_b, dt_bias, a_log, d_skip, ssd_norm_w, pool_w, pool_scale, ret_gn_w, sb_q_norm, sb_k_norm, w_out, norm2_w, w_ff1, w_ff2):
    bp, seq, _ = x_prompt.shape
    bs = x_sample.shape[0]
    n_phys = cache_k.shape[1]
    past_len = page_table.shape[1] * PAGE_SIZE
    ncp = seq // CHUNK
    xp = x_prompt.reshape(bp * seq, D_MODEL)
    xs = x_sample.reshape(bs, D_MODEL)
    ck = cache_k.reshape(DEPTH * n_phys, PAGE_SIZE * SB_HEADS, HEAD_DIM)
    cv = cache_v.reshape(DEPTH * n_phys, PAGE_SIZE * SB_HEADS, HEAD_DIM)
    zeros = lambda *shape: jnp.zeros(shape, F32)
    outs = {name: [] for name in ("sp", "ss", "cp", "cs", "pp", "ps", "rp", "rs")}
    kv_p = kv_s = None

    w_in = jnp.swapaxes(w_in, 1, 2)
    wa, wdt, wb, w1 = convert_weights([("w_in", w_in, 0), ("plain", w_ff1, 0)])
    padded = lambda a: jnp.pad(a[:, None, :], ((0, 0), (0, CHUNK - 1), (0, 0))).reshape(bs * CHUNK, a.shape[1])
    first = lambda y: y.reshape(bs, CHUNK, -1)[:, 0, :]

    for l in range(DEPTH):
        ssd_w = (conv_w[l], conv_b[l], dt_bias[l], a_log[l], d_skip[l], ssd_norm_w[l])
        (pa, dt_raw, pb), (pa_s, dt_s, pb_s), wo = _dense_in(l, xp, xs, norm1_w[l], wa, wdt, wb, w_out)

        mix_p, cp, sp = ssd_mixer(pa, dt_raw, zeros(bp, 8, SSD_CONV_CH), zeros(bp, SSD_HEADS, 64, SSD_STATE), *ssd_w,
                                  nb=bp, nc=ncp, n_valid=CHUNK, mix="new")
        mix_p, pp = pool_mixer(pb, zeros(bp, POOL_BUF + 1, W_POOL), pool_w[l], pool_scale[l],
                               nb=bp, nt=seq // 256, tl=256, n_valid=256, start=0, mix=mix_p)
        mix_p, rp = retention_mixer(pb, zeros(bp, RET_HEADS, HEAD_DIM, HEAD_DIM), ret_gn_w[l],
                                    nb=bp, nc=ncp, n_valid=CHUNK, start=0, mix=mix_p)
        qn, kb, vb, kv_p = sb_prep(pb, sb_q_norm[l], sb_k_norm[l], l, kv_p)
        mix_p = sb_prompt_attention(qn, kb, vb, nb=bp, seq=seq, mix=mix_p)
        outs["sp"].append(sp); outs["cp"].append(cp); outs["pp"].append(pp); outs["rp"].append(rp)

        pa_sp, dt_sp, pb_sp = padded(pa_s), padded(dt_s), padded(pb_s)
        y_ssd, cs, ss = ssd_mixer(pa_sp, dt_sp, jnp.pad(state_conv[l], ((0, 0), (5, 0), (0, 0))), state_ssm[l], *ssd_w,
                                  nb=bs, nc=1, n_valid=1)
        y_pool, ps = pool_mixer(pb_sp, jnp.pad(state_pool[l], ((0, 0), (1, 0), (0, 0))), pool_w[l], pool_scale[l],
                                nb=bs, nt=1, tl=CHUNK, n_valid=1, start=past_len)
        y_ret, rs = retention_mixer(pb_sp, state_ret[l], ret_gn_w[l], nb=bs, nc=1, n_valid=1, start=past_len)
        qn, kb, vb, kv_s = sb_prep(pb_s, sb_q_norm[l], sb_k_norm[l], l, kv_s)
        y_sb = sb_decode_attention(qn, ck, cv, page_table + l * n_phys)
        mix_s = jnp.concatenate([first(y_ssd), first(y_pool), first(y_ret), y_sb], axis=1)
        outs["ss"].append(ss); outs["cs"].append(cs); outs["ps"].append(ps); outs["rs"].append(rs)

        xp, xs, w1_next, w_in_next = _dense_out(l, xp, xs, mix_p, mix_s, wo, norm2_w[l], w1, w_in, w_ff1, w_ff2)
        if l + 1 < DEPTH:
            (w1,), (wa, wdt, wb) = w1_next, w_in_next

    st = lambda name: jnp.stack(outs[name])
    kv = lambda a, nb, n: a.reshape(DEPTH, nb, n, SB_HEADS, HEAD_DIM)
    return (xp.reshape(bp, seq, D_MODEL), xs.reshape(bs, 1, D_MODEL),
            kv(kv_p[0], bp, seq), kv(kv_p[1], bp, seq), kv(kv_s[0], bs, 1), kv(kv_s[1], bs, 1),
            st("sp"), st("ss"), st("cp"), st("cs"), st("pp"), st("ps"), st("rp"), st("rs"))
```
